```python
import math
import jax, jax.numpy as jnp
from jax import lax
import numpy as np

D_MODEL = 1024
BATCH = 16
SEQ = 2048
DEPTH = 4
DEC_BATCH = 4
DEC_SEQ = 8192
PAST_LEN = 128

GRID_W = 64
QBLK = 128
EPS = 1e-6
N_EVEN = (DEPTH + 1) // 2
N_ODD = DEPTH // 2
D_FF = 2816
ROPE_THETA = 500000.0
AXIAL_THETA = 10000.0
MLA_HEADS = 8
MLA_NOPE = 64
MLA_ROPE = 32
MLA_V = 64
MLA_Q_RANK = 384
MLA_KV_RANK = 256
POOL_WINDOWS = (2, 4, 8, 16)
POOL_GROUP = 128
POOL_WIDTH = POOL_GROUP * len(POOL_WINDOWS)
GQA_HEADS = 8
GQA_KV_HEADS = 2
GQA_DIM = 64
DIFF_HEADS = 4
DIFF_DIM = 64
DIFF_ROPE = DIFF_DIM // 4

EVEN_SIZES = (MLA_Q_RANK, MLA_KV_RANK, MLA_ROPE, POOL_WIDTH)
ODD_SIZES = (GQA_HEADS * GQA_DIM, GQA_KV_HEADS * GQA_DIM, GQA_KV_HEADS * GQA_DIM,
             DIFF_HEADS * 2 * DIFF_DIM, DIFF_HEADS * 2 * DIFF_DIM, DIFF_HEADS * 2 * DIFF_DIM)
EVEN_IN = sum(EVEN_SIZES)
ODD_IN = sum(ODD_SIZES)
EVEN_MIX = MLA_HEADS * MLA_V + POOL_WIDTH
ODD_MIX = GQA_HEADS * GQA_DIM + DIFF_HEADS * 2 * DIFF_DIM

kernel_name = "hybrid_bidir_encoder_mla_pool_gqa_diff"


def _split(z, sizes):
    out, o = [], 0
    for n in sizes:
        out.append(z[..., o:o + n])
        o += n
    return out


def _rmsnorm(x, g):
    xf = x.astype(jnp.float32)
    y = xf * lax.rsqrt(jnp.mean(xf * xf, axis=-1, keepdims=True) + EPS)
    return y.astype(x.dtype) * g


def _rope(x, pos, theta):
    d = x.shape[-1]
    inv = theta ** (-jnp.arange(0, d // 2, dtype=jnp.float32) * 2.0 / d)
    ang = pos.astype(jnp.float32)[:, None] * inv[None, :]
    cos = jnp.cos(ang)[None, :, None, :].astype(x.dtype)
    sin = jnp.sin(ang)[None, :, None, :].astype(x.dtype)
    x1, x2 = x[..., :d // 2], x[..., d // 2:]
    return jnp.concatenate([x1 * cos - x2 * sin, x1 * sin + x2 * cos], axis=-1)


def _partial_rope(x, pos, rd, theta):
    return jnp.concatenate([_rope(x[..., :rd], pos, theta), x[..., rd:]], axis=-1)


def _axial_rope(x, rows, cols, theta):
    h = x.shape[-1] // 2
    return jnp.concatenate([_rope(x[..., :h], rows, theta), _rope(x[..., h:], cols, theta)], axis=-1)


def _to_blocks(a):
    b, s = a.shape[:2]
    return jnp.swapaxes(a.reshape((b, s // QBLK, QBLK) + a.shape[2:]), 0, 1)


def _from_blocks(a):
    nb, b, q = a.shape[:3]
    return jnp.swapaxes(a, 0, 1).reshape((b, nb * q) + a.shape[3:])


def _gqa_attention(q, k, v, scale):
    def blk(qb):
        s = jnp.einsum('bqkgd,bskd->bkgqs', qb, k, preferred_element_type=jnp.float32) * scale
        p = jax.nn.softmax(s, axis=-1).astype(v.dtype)
        return jnp.einsum('bkgqs,bskd->bqkgd', p, v)
    return _from_blocks(lax.map(blk, _to_blocks(q)))


def _diff_attention(q1, q2, k1, k2, v, lam, scale):
    def blk(qs):
        qb1, qb2 = qs
        s1 = jnp.einsum('bqhd,bshd->bhqs', qb1, k1, preferred_element_type=jnp.float32) * scale
        s2 = jnp.einsum('bqhd,bshd->bhqs', qb2, k2, preferred_element_type=jnp.float32) * scale
        a = jax.nn.softmax(s1, axis=-1) - lam * jax.nn.softmax(s2, axis=-1)
        return jnp.einsum('bhqs,bshd->bqhd', a.astype(v.dtype), v)
    return _from_blocks(lax.map(blk, (_to_blocks(q1), _to_blocks(q2))))


def _multiscale_pool(p, w_pool, scale):
    b, s, _ = p.shape
    pf = p.astype(jnp.float32)
    cs = jnp.concatenate([jnp.zeros((b, 1, POOL_WIDTH), jnp.float32), jnp.cumsum(pf, axis=1)], axis=1)
    t = jnp.arange(s)
    outs = []
    for g, w in enumerate(POOL_WINDOWS):
        lo = jnp.clip(t - w // 2, 0, s)
        hi = jnp.clip(t + w - w // 2, 0, s)
        sl = slice(g * POOL_GROUP, (g + 1) * POOL_GROUP)
        csg = cs[..., sl]
        mean = (csg[:, hi] - csg[:, lo]) / (hi - lo).astype(jnp.float32)[None, :, None]
        d = (mean - pf[..., sl]).astype(p.dtype)
        outs.append(jnp.einsum('bsc,cd->bsd', d, w_pool[g]))
    return jnp.concatenate(outs, axis=-1) * scale


def _swiglu(h, w_in, w_out):
    g, u = _split(h @ w_in, (D_FF, D_FF))
    return (jax.nn.silu(g) * u) @ w_out


def _even_mixer(h, pos, w_in, gq, w_uq, gkv, w_ukv, w_pool, pool_scale, w_out):
    b, s, _ = h.shape
    cq, ckv, kpe, pz = _split(h @ w_in, EVEN_SIZES)
    q = (_rmsnorm(cq, gq) @ w_uq).reshape(b, s, MLA_HEADS, MLA_NOPE + MLA_ROPE)
    kv = (_rmsnorm(ckv, gkv) @ w_ukv).reshape(b, s, MLA_HEADS, MLA_NOPE + MLA_V)
    q = jnp.concatenate([q[..., :MLA_NOPE], _rope(q[..., MLA_NOPE:], pos, ROPE_THETA)], axis=-1)
    kpe = _rope(kpe[:, :, None, :], pos, ROPE_THETA)
    k = jnp.concatenate([kv[..., :MLA_NOPE], jnp.broadcast_to(kpe, (b, s, MLA_HEADS, MLA_ROPE))], axis=-1)
    v = kv[..., MLA_NOPE:]
    oa = _gqa_attention(q[:, :, :, None, :], k, v, (MLA_NOPE + MLA_ROPE) ** -0.5)
    oa = oa.reshape(b, s, MLA_HEADS * MLA_V)
    ob = _multiscale_pool(pz, w_pool, pool_scale)
    return jnp.concatenate([oa, ob], axis=-1) @ w_out


def _odd_mixer(h, pos, rows, cols, layer_idx, w_in, gqa_gq, gqa_gk, lq1, lk1, lq2, lk2, subln_g, w_out):
    b, s, _ = h.shape
    qc, kc, vc, qd, kd, vd = _split(h @ w_in, ODD_SIZES)
    qc = _rmsnorm(qc.reshape(b, s, GQA_HEADS, GQA_DIM), gqa_gq)
    kc = _rmsnorm(kc.reshape(b, s, GQA_KV_HEADS, GQA_DIM), gqa_gk)
    qc = _axial_rope(qc, rows, cols, AXIAL_THETA).reshape(b, s, GQA_KV_HEADS, GQA_HEADS // GQA_KV_HEADS, GQA_DIM)
    kc = _axial_rope(kc, rows, cols, AXIAL_THETA)
    vc = vc.reshape(b, s, GQA_KV_HEADS, GQA_DIM)
    oc = _gqa_attention(qc, kc, vc, GQA_DIM ** -0.5).reshape(b, s, GQA_HEADS * GQA_DIM)
    qd = _partial_rope(qd.reshape(b, s, DIFF_HEADS * 2, DIFF_DIM), pos, DIFF_ROPE, ROPE_THETA)
    kd = _partial_rope(kd.reshape(b, s, DIFF_HEADS * 2, DIFF_DIM), pos, DIFF_ROPE, ROPE_THETA)
    qd = qd.reshape(b, s, DIFF_HEADS, 2, DIFF_DIM)
    kd = kd.reshape(b, s, DIFF_HEADS, 2, DIFF_DIM)
    vd = vd.reshape(b, s, DIFF_HEADS, 2 * DIFF_DIM)
    lam_init = 0.8 - 0.6 * math.exp(-0.3 * layer_idx)
    f32 = jnp.float32
    lam = (jnp.exp(jnp.sum(lq1.astype(f32) * lk1.astype(f32)))
           - jnp.exp(jnp.sum(lq2.astype(f32) * lk2.astype(f32))) + lam_init)
    od = _diff_attention(qd[:, :, :, 0], qd[:, :, :, 1], kd[:, :, :, 0], kd[:, :, :, 1], vd, lam, DIFF_DIM ** -0.5)
    od = _rmsnorm(od, subln_g) * (1.0 - lam_init)
    return jnp.concatenate([oc, od.reshape(b, s, DIFF_HEADS * 2 * DIFF_DIM)], axis=-1) @ w_out


def _forward(x, c, p):
    b, s, _ = x.shape
    n_rows = s // GRID_W
    pos = jnp.arange(s)
    rows = jnp.repeat(jnp.arange(n_rows), GRID_W)
    cols = pos % GRID_W
    sc = jax.nn.silu(c)
    for i in range(DEPTH):
        mod = (sc @ p['w_ada'][i] + p['b_ada'][i])[:, None, :]
        sh0, s0, g0, sh1, s1, g1, sh2, s2, g2 = jnp.split(mod, 9, axis=-1)
        h = _rmsnorm(x, p['norm_g'][i, 0]) * (1 + s0) + sh0
        x = x + 0.5 * g0 * _swiglu(h, p['ffn_w_in'][i, 0], p['ffn_w_out'][i, 0])
        h = _rmsnorm(x, p['norm_g'][i, 1]) * (1 + s1) + sh1
        if i % 2 == 0:
            j = i // 2
            m = _even_mixer(h, pos, p['ev_w_in'][j], p['mla_gq'][j], p['mla_w_uq'][j], p['mla_gkv'][j],
                            p['mla_w_ukv'][j], p['pool_w'][j], p['pool_scale'][j], p['ev_w_out'][j])
        else:
            j = i // 2
            m = _odd_mixer(h, pos, rows, cols, i, p['od_w_in'][j], p['gqa_gq'][j], p['gqa_gk'][j],
                           p['diff_lq1'][j], p['diff_lk1'][j], p['diff_lq2'][j], p['diff_lk2'][j],
                           p['diff_subln_g'][j], p['od_w_out'][j])
        x = x + g1 * m
        h = _rmsnorm(x, p['norm_g'][i, 2]) * (1 + s2) + sh2
        x = x + 0.5 * g2 * _swiglu(h, p['ffn_w_in'][i, 1], p['ffn_w_out'][i, 1])
    return _rmsnorm(x, p['final_g'])


def setup_inputs(seed: int = 0) -> dict:
    key = jax.random.key(seed)
    ks = jax.random.split(key, 32)
    f32 = jnp.float32

    def nrm(k, shape, fan):
        return jax.random.normal(k, shape, f32) * (fan ** -0.5)

    def gain(k, shape):
        return 1.0 + 0.02 * jax.random.normal(k, shape, f32)

    D = D_MODEL
    return {
        'x_prompt': jax.random.normal(ks[0], (BATCH, SEQ, D), f32),
        'x_sample': jax.random.normal(ks[1], (DEC_BATCH, DEC_SEQ, D), f32),
        'c_prompt': jax.random.normal(ks[2], (BATCH, D), f32),
        'c_sample': jax.random.normal(ks[3], (DEC_BATCH, D), f32),
        'w_ada': nrm(ks[4], (DEPTH, D, 9 * D), D),
        'b_ada': 0.01 * jax.random.normal(ks[5], (DEPTH, 9 * D), f32),
        'norm_g': gain(ks[6], (DEPTH, 3, D)),
        'ffn_w_in': nrm(ks[7], (DEPTH, 2, D, 2 * D_FF), D),
        'ffn_w_out': nrm(ks[8], (DEPTH, 2, D_FF, D), D_FF),
        'ev_w_in': nrm(ks[9], (N_EVEN, D, EVEN_IN), D),
        'mla_gq': gain(ks[10], (N_EVEN, MLA_Q_RANK)),
        'mla_w_uq': nrm(ks[11], (N_EVEN, MLA_Q_RANK, MLA_HEADS * (MLA_NOPE + MLA_ROPE)), MLA_Q_RANK),
        'mla_gkv': gain(ks[12], (N_EVEN, MLA_KV_RANK)),
        'mla_w_ukv': nrm(ks[13], (N_EVEN, MLA_KV_RANK, MLA_HEADS * (MLA_NOPE + MLA_V)), MLA_KV_RANK),
        'pool_w': nrm(ks[14], (N_EVEN, len(POOL_WINDOWS), POOL_GROUP, POOL_GROUP), POOL_GROUP),
        'pool_scale': gain(ks[15], (N_EVEN, POOL_WIDTH)),
        'ev_w_out': nrm(ks[16], (N_EVEN, EVEN_MIX, D), EVEN_MIX),
        'od_w_in': nrm(ks[17], (N_ODD, D, ODD_IN), D),
        'gqa_gq': gain(ks[18], (N_ODD, GQA_DIM)),
        'gqa_gk': gain(ks[19], (N_ODD, GQA_DIM)),
        'diff_lq1': 0.1 * jax.random.normal(ks[20], (N_ODD, DIFF_DIM), f32),
        'diff_lk1': 0.1 * jax.random.normal(ks[21], (N_ODD, DIFF_DIM), f32),
        'diff_lq2': 0.1 * jax.random.normal(ks[22], (N_ODD, DIFF_DIM), f32),
        'diff_lk2': 0.1 * jax.random.normal(ks[23], (N_ODD, DIFF_DIM), f32),
        'diff_subln_g': gain(ks[24], (N_ODD, 2 * DIFF_DIM)),
        'od_w_out': nrm(ks[25], (N_ODD, ODD_MIX, D), ODD_MIX),
        'final_g': gain(ks[26], (D,)),
    }


def reference(x_prompt, x_sample, c_prompt, c_sample, w_ada, b_ada, norm_g, ffn_w_in, ffn_w_out,
              ev_w_in, mla_gq, mla_w_uq, mla_gkv, mla_w_ukv, pool_w, pool_scale, ev_w_out,
              od_w_in, gqa_gq, gqa_gk, diff_lq1, diff_lk1, diff_lq2, diff_lk2, diff_subln_g, od_w_out,
              final_g):
    p = {
        'w_ada': w_ada, 'b_ada': b_ada, 'norm_g': norm_g, 'ffn_w_in': ffn_w_in, 'ffn_w_out': ffn_w_out,
        'ev_w_in': ev_w_in, 'mla_gq': mla_gq, 'mla_w_uq': mla_w_uq, 'mla_gkv': mla_gkv,
        'mla_w_ukv': mla_w_ukv, 'pool_w': pool_w, 'pool_scale': pool_scale, 'ev_w_out': ev_w_out,
        'od_w_in': od_w_in, 'gqa_gq': gqa_gq, 'gqa_gk': gqa_gk, 'diff_lq1': diff_lq1,
        'diff_lk1': diff_lk1, 'diff_lq2': diff_lq2, 'diff_lk2': diff_lk2,
        'diff_subln_g': diff_subln_g, 'od_w_out': od_w_out, 'final_g': final_g,
    }
    y_prompt = _forward(x_prompt, c_prompt, p)
    y_sample = _forward(x_sample, c_sample, p)
    return (y_prompt, y_sample)
```

```python
import functools
import math

import jax
import jax.numpy as jnp
from jax import lax
from jax.experimental import pallas as pl
from jax.experimental.pallas import tpu as pltpu

F32 = jnp.float32
BF16 = jnp.bfloat16

D_MODEL = 1024
DEPTH = 4
GRID_W = 64
EPS = 1e-6
D_FF = 2816
ROPE_THETA = 500000.0
AXIAL_THETA = 10000.0
MLA_HEADS = 8
MLA_NOPE = 64
MLA_ROPE = 32
MLA_V = 64
MLA_Q_RANK = 384
MLA_KV_RANK = 256
POOL_WINDOWS = (2, 4, 8, 16)
POOL_GROUP = 128
POOL_WIDTH = POOL_GROUP * len(POOL_WINDOWS)
POOL_HALO = 8
GQA_HEADS = 8
GQA_KV_HEADS = 2
GQA_DIM = 64
DIFF_HEADS = 4
DIFF_DIM = 64
DIFF_ROPE = DIFF_DIM // 4
LANES = 128
LOG2E = math.log2(math.e)

TOKEN_TILE = 512
Q_TILE = 256
KV_CHUNK = 512
ADA_COL_TILE = 1536
FFN_CHUNKS = ((0, 1024), (1024, 1024), (2048, 768))
VMEM_LIMIT = 56 * 1024 * 1024


def _params(n_grid):
    return pltpu.CompilerParams(dimension_semantics=("arbitrary",) * n_grid, vmem_limit_bytes=VMEM_LIMIT)


def _const_spec(shape):
    nd = len(shape)
    return pl.BlockSpec(shape, lambda *_: (0,) * nd, pipeline_mode=pl.Buffered(1))


def _mod_spec(layer, chunk):
    return pl.BlockSpec((None, None, 1, D_MODEL), lambda b, *_: (layer, b, 0, chunk))


def _dot(a, b):
    return jnp.dot(a, b, preferred_element_type=F32)


def _rms(x):
    return x * lax.rsqrt(jnp.mean(x * x, axis=-1, keepdims=True) + EPS)


def _modnorm(x, g, scale, shift):
    return (_rms(x) * g) * (1.0 + scale) + shift


def _silu(x):
    return x * jax.nn.sigmoid(x)


def _rope(x, tab_ref, half):
    width = x.shape[-1]
    fwd = pltpu.roll(x, width - half, axis=1)
    back = pltpu.roll(x, half, axis=1)
    return x * tab_ref[0] + fwd * tab_ref[1] + back * tab_ref[2]


def _ada_kernel(c_ref, w_ref, b_ref, o_ref):
    sc = _silu(c_ref[...]).astype(BF16)
    o_ref[0] = _dot(sc, w_ref[0].astype(BF16)) + b_ref[0]


def _ada_call(c_all, w_ada, b_ada):
    rows = c_all.shape[0]
    n_out = w_ada.shape[-1]
    return pl.pallas_call(
        _ada_kernel,
        grid=(DEPTH, n_out // ADA_COL_TILE),
        in_specs=[
            pl.BlockSpec((rows, D_MODEL), lambda i, n: (0, 0)),
            pl.BlockSpec((1, D_MODEL, ADA_COL_TILE), lambda i, n: (i, 0, n)),
            pl.BlockSpec((1, 1, ADA_COL_TILE), lambda i, n: (i, 0, n)),
        ],
        out_specs=pl.BlockSpec((1, rows, ADA_COL_TILE), lambda i, n: (i, 0, n)),
        out_shape=jax.ShapeDtypeStruct((DEPTH, rows, n_out), F32),
        compiler_params=_params(2),
        name="ada_mod",
    )(c_all, w_ada, b_ada.reshape(DEPTH, 1, n_out))


def _ffn_kernel(*refs, has_mix, has_final):
    it = iter(refs)
    x_ref = next(it)
    if has_mix:
        ma_ref, mb_ref, wmix_ref, g1_ref = next(it), next(it), next(it), next(it)
    sh_ref, sc_ref, gt_ref, ng_ref, win_ref, wout_ref = (next(it) for _ in range(6))
    if has_final:
        fg_ref = next(it)
    o_ref = next(it)

    x = x_ref[0]
    if has_mix:
        half = wmix_ref.shape[0] // 2
        m = _dot(ma_ref[0], wmix_ref[:half, :]) + _dot(mb_ref[0], wmix_ref[half:, :])
        x = x + g1_ref[...] * m
    h = _modnorm(x, ng_ref[...], sc_ref[...], sh_ref[...]).astype(BF16)
    acc = None
    for c0, cn in FFN_CHUNKS:
        g = _dot(h, win_ref[:, c0:c0 + cn])
        u = _dot(h, win_ref[:, D_FF + c0:D_FF + c0 + cn])
        a = (_silu(g) * u).astype(BF16)
        d = _dot(a, wout_ref[c0:c0 + cn, :])
        acc = d if acc is None else acc + d
    y = x + (0.5 * gt_ref[...]) * acc
    if has_final:
        y = _rms(y) * fg_ref[...]
    o_ref[0] = y


def _ffn_call(x, mod, layer, which, norm_g_row, w_in, w_out, mix=None, final_g=None):
    bsz, seq, _ = x.shape
    tm = TOKEN_TILE
    base = 0 if which == 0 else 6
    tok = pl.BlockSpec((1, tm, D_MODEL), lambda b, t: (b, t, 0))
    args, specs = [x], [tok]
    if mix is not None:
        ma, mb, wmix = mix
        half = pl.BlockSpec((1, tm, D_MODEL // 2), lambda b, t: (b, t, 0))
        args += [ma, mb, wmix, mod]
        specs += [half, half, _const_spec(wmix.shape), _mod_spec(layer, 5)]
    args += [mod, mod, mod, norm_g_row, w_in, w_out]
    specs += [_mod_spec(layer, base), _mod_spec(layer, base + 1), _mod_spec(layer, base + 2),
              _const_spec(norm_g_row.shape), _const_spec(w_in.shape), _const_spec(w_out.shape)]
    if final_g is not None:
        args.append(final_g)
        specs.append(_const_spec(final_g.shape))
    return pl.pallas_call(
        functools.partial(_ffn_kernel, has_mix=mix is not None, has_final=final_g is not None),
        grid=(bsz, seq // tm),
        in_specs=specs,
        out_specs=tok,
        out_shape=jax.ShapeDtypeStruct(x.shape, F32),
        compiler_params=_params(2),
        name="ffn",
    )(*args)


def _pool(pz, halo, t_idx, n_tiles, seq, poolw_ref, pscale_ref):
    tm = pz.shape[0]
    prev = jnp.where(t_idx > 0, halo[:POOL_HALO], 0.0)
    nxt = jnp.where(t_idx < n_tiles - 1, halo[POOL_HALO:], 0.0)
    ext = jnp.concatenate([prev, pz, nxt], axis=0)
    n_ext = ext.shape[0]
    pos = t_idx * tm + lax.broadcasted_iota(jnp.int32, (tm, 1), 0)
    outs = []
    for g, w in enumerate(POOL_WINDOWS):
        sl = slice(g * POOL_GROUP, (g + 1) * POOL_GROUP)
        run = ext[:, sl]
        span = 1
        while span < w:
            run = run + pltpu.roll(run, n_ext - span, axis=0)
            span *= 2
        start = POOL_HALO - w // 2
        if start:
            run = pltpu.roll(run, n_ext - start, axis=0)
        win = run[:tm]
        cnt = jnp.minimum(pos + (w - w // 2), seq) - jnp.maximum(pos - w // 2, 0)
        d = (win / cnt.astype(F32) - pz[:, sl]).astype(BF16)
        outs.append(_dot(d, poolw_ref[g]))
    return jnp.concatenate(outs, axis=-1) * pscale_ref[...]


def _even_in_kernel(x_ref, xp_ref, xn_ref, sh_ref, sc_ref, ng_ref, wz_ref, gq_ref, wuq_ref, gkv_ref, wkv_ref,
                    tab_ref, poolw_ref, pscale_ref, q_ref, k_ref, v_ref, ob_ref, *, seq):
    t_idx = pl.program_id(1)
    n_tiles = pl.num_programs(1)
    g, sc, sh = ng_ref[...], sc_ref[...], sh_ref[...]
    h = _modnorm(x_ref[0], g, sc, sh).astype(BF16)
    z = _dot(h, wz_ref[...])
    o_kpe = MLA_Q_RANK + MLA_KV_RANK
    o_pz = o_kpe + LANES

    xh = jnp.concatenate([xp_ref[0], xn_ref[0]], axis=0)
    hh = _modnorm(xh, g, sc, sh).astype(BF16)
    halo = _dot(hh, wz_ref[:, o_pz:])
    ob_ref[0] = _pool(z[:, o_pz:], halo, t_idx, n_tiles, seq, poolw_ref, pscale_ref).astype(BF16)

    cq = (_rms(z[:, :MLA_Q_RANK]) * gq_ref[...]).astype(BF16)
    ckv = (_rms(z[:, MLA_Q_RANK:o_kpe]) * gkv_ref[...]).astype(BF16)
    q_all = _dot(cq, wuq_ref[...])
    kv_all = _dot(ckv, wkv_ref[...])
    kpe = _rope(z[:, o_kpe:o_pz], tab_ref, MLA_ROPE // 2)
    for hd in range(MLA_HEADS):
        sl = slice(hd * LANES, (hd + 1) * LANES)
        q_ref[0, hd] = _rope(q_all[:, sl], tab_ref, MLA_ROPE // 2).astype(BF16)
        k_ref[0, hd] = (kv_all[:, sl] + kpe).astype(BF16)
    v_ref[0] = kv_all[:, MLA_HEADS * LANES:].astype(BF16)


def _even_in_call(x, mod, layer, norm_g_row, wz, gq, wuq, gkv, wkv, tab, poolw, pscale):
    bsz, seq, _ = x.shape
    tm = TOKEN_TILE
    hb = tm // POOL_HALO
    n_hblk = seq // POOL_HALO
    tok = pl.BlockSpec((1, tm, D_MODEL), lambda b, t: (b, t, 0))
    prev = pl.BlockSpec((1, POOL_HALO, D_MODEL), lambda b, t: (b, jnp.maximum(t * hb - 1, 0), 0))
    nxt = pl.BlockSpec((1, POOL_HALO, D_MODEL), lambda b, t: (b, jnp.minimum((t + 1) * hb, n_hblk - 1), 0))
    heads = pl.BlockSpec((1, MLA_HEADS, tm, LANES), lambda b, t: (b, 0, t, 0))
    half = pl.BlockSpec((1, tm, D_MODEL // 2), lambda b, t: (b, t, 0))
    out_heads = jax.ShapeDtypeStruct((bsz, MLA_HEADS, seq, LANES), BF16)
    out_half = jax.ShapeDtypeStruct((bsz, seq, D_MODEL // 2), BF16)
    return pl.pallas_call(
        functools.partial(_even_in_kernel, seq=seq),
        grid=(bsz, seq // tm),
        in_specs=[tok, prev, nxt, _mod_spec(layer, 3), _mod_spec(layer, 4), _const_spec(norm_g_row.shape),
                  _const_spec(wz.shape), _const_spec(gq.shape), _const_spec(wuq.shape), _const_spec(gkv.shape),
                  _const_spec(wkv.shape), pl.BlockSpec((3, tm, LANES), lambda b, t: (0, t, 0)),
                  _const_spec(poolw.shape), _const_spec(pscale.shape)],
        out_specs=[heads, heads, half, half],
        out_shape=[out_heads, out_heads, out_half, out_half],
        compiler_params=_params(2),
        name="even_in",
    )(x, x, x, mod, mod, norm_g_row, wz, gq, wuq, gkv, wkv, tab, poolw, pscale)


def _pair_norm(x, gain):
    lo_mask = lax.broadcasted_iota(jnp.int32, x.shape, 1) < GQA_DIM
    sq = x * x
    lo = jnp.sum(jnp.where(lo_mask, sq, 0.0), axis=-1, keepdims=True)
    hi = jnp.sum(jnp.where(lo_mask, 0.0, sq), axis=-1, keepdims=True)
    inv = jnp.where(lo_mask, lax.rsqrt(lo / GQA_DIM + EPS), lax.rsqrt(hi / GQA_DIM + EPS))
    return x * inv * gain


def _odd_in_kernel(x_ref, sh_ref, sc_ref, ng_ref, wz_ref, gq_ref, gk_ref, tabc_ref, tabd_ref,
                   qc_ref, kc_ref, vc_ref, qd_ref, kd_ref, vd_ref):
    h = _modnorm(x_ref[0], ng_ref[...], sc_ref[...], sh_ref[...]).astype(BF16)
    z = _dot(h, wz_ref[...])
    n_qc = GQA_HEADS * GQA_DIM // LANES
    n_kv = GQA_KV_HEADS
    n_d = DIFF_HEADS
    col = 0
    for p in range(n_qc):
        blk = z[:, col:col + LANES]
        qc_ref[0, p] = _rope(_pair_norm(blk, gq_ref[...]), tabc_ref, GQA_DIM // 4).astype(BF16)
        col += LANES
    for j in range(n_kv):
        blk = z[:, col:col + LANES]
        kc_ref[0, j] = _rope(_pair_norm(blk, gk_ref[...]), tabc_ref, GQA_DIM // 4).astype(BF16)
        col += LANES
    for j in range(n_kv):
        vc_ref[0, j] = z[:, col:col + LANES].astype(BF16)
        col += LANES
    for hd in range(n_d):
        qd_ref[0, hd] = _rope(z[:, col:col + LANES], tabd_ref, DIFF_ROPE // 2).astype(BF16)
        col += LANES
    for hd in range(n_d):
        kd_ref[0, hd] = _rope(z[:, col:col + LANES], tabd_ref, DIFF_ROPE // 2).astype(BF16)
        col += LANES
    vd_ref[0] = z[:, col:].astype(BF16)


def _odd_in_call(x, mod, layer, norm_g_row, wz, gq2, gk2, tabc, tabd):
    bsz, seq, _ = x.shape
    tm = TOKEN_TILE
    tok = pl.BlockSpec((1, tm, D_MODEL), lambda b, t: (b, t, 0))
    tab = pl.BlockSpec((3, tm, LANES), lambda b, t: (0, t, 0))

    def blocks(n):
        return (pl.BlockSpec((1, n, tm, LANES), lambda b, t: (b, 0, t, 0)),
                jax.ShapeDtypeStruct((bsz, n, seq, LANES), BF16))

    n_qc = GQA_HEADS * GQA_DIM // LANES
    outs = [blocks(n_qc), blocks(GQA_KV_HEADS), blocks(GQA_KV_HEADS), blocks(DIFF_HEADS), blocks(DIFF_HEADS),
            (pl.BlockSpec((1, tm, D_MODEL // 2), lambda b, t: (b, t, 0)),
             jax.ShapeDtypeStruct((bsz, seq, D_MODEL // 2), BF16))]
    return pl.pallas_call(
        _odd_in_kernel,
        grid=(bsz, seq // tm),
        in_specs=[tok, _mod_spec(layer, 3), _mod_spec(layer, 4), _const_spec(norm_g_row.shape),
                  _const_spec(wz.shape), _const_spec(gq2.shape), _const_spec(gk2.shape), tab, tab],
        out_specs=[o[0] for o in outs],
        out_shape=[o[1] for o in outs],
        compiler_params=_params(2),
        name="odd_in",
    )(x, mod, mod, norm_g_row, wz, gq2, gk2, tabc, tabd)


def _flash(streams, v_ref, scale, seq):
    c = scale * LOG2E
    tk = min(KV_CHUNK, seq)

    def body(j, carry):
        off = pl.multiple_of(j * tk, tk)
        vc = v_ref[pl.ds(off, tk), :]
        new = []
        for (q, k_ref), (m, l, acc) in zip(streams, carry):
            kc = k_ref[pl.ds(off, tk), :]
            s = lax.dot_general(q, kc, (((1,), (1,)), ((), ())), preferred_element_type=F32)
            m_new = jnp.maximum(m, jnp.max(s, axis=-1, keepdims=True))
            alpha = jnp.exp2((m - m_new) * c)
            p = jnp.exp2(s * c - m_new * c)
            l_new = alpha * l + jnp.sum(p, axis=-1, keepdims=True)
            acc_new = alpha * acc + _dot(p.astype(BF16), vc)
            new.append((m_new, l_new, acc_new))
        return tuple(new)

    init = tuple((jnp.full((q.shape[0], 1), -jnp.inf, F32), jnp.zeros((q.shape[0], 1), F32),
                  jnp.zeros((q.shape[0], LANES), F32)) for q, _ in streams)
    out = lax.fori_loop(0, seq // tk, body, init)
    return [(acc, l) for _, l, acc in out]


def _lo_mask(shape):
    return lax.broadcasted_iota(jnp.int32, shape, 1) < LANES // 2


def _mla_attn_kernel(q_ref, k_ref, v_ref, o_ref, *, seq):
    scale = (MLA_NOPE + MLA_ROPE) ** -0.5
    (a0, l0), (a1, l1) = _flash([(q_ref[0, 0], k_ref.at[0, 0]), (q_ref[0, 1], k_ref.at[0, 1])],
                                v_ref.at[0], scale, seq)
    o_ref[0] = jnp.where(_lo_mask(a0.shape), a0 / l0, a1 / l1).astype(BF16)


def _mla_attn_call(q, k, v):
    bsz, _, seq, _ = q.shape
    tq = Q_TILE
    return pl.pallas_call(
        functools.partial(_mla_attn_kernel, seq=seq),
        grid=(bsz, MLA_HEADS // 2, seq // tq),
        in_specs=[pl.BlockSpec((1, 2, tq, LANES), lambda b, p, i: (b, p, i, 0)),
                  pl.BlockSpec((1, 2, seq, LANES), lambda b, p, i: (b, p, 0, 0)),
                  pl.BlockSpec((1, seq, LANES), lambda b, p, i: (b, 0, p))],
        out_specs=pl.BlockSpec((1, tq, LANES), lambda b, p, i: (b, i, p)),
        out_shape=jax.ShapeDtypeStruct((bsz, seq, D_MODEL // 2), BF16),
        compiler_params=_params(3),
        name="mla_attn",
    )(q, k, v)


def _gqa_attn_kernel(q_ref, k_ref, v_ref, o_ref, *, seq):
    n_pairs = q_ref.shape[1]
    tq = q_ref.shape[2]
    parts = []
    for p in range(n_pairs):
        qp = q_ref[0, p]
        lo = _lo_mask(qp.shape)
        parts += [jnp.where(lo, qp, 0), jnp.where(lo, 0, qp)]
    q_all = jnp.concatenate(parts, axis=0).astype(BF16)
    ((acc, l),) = _flash([(q_all, k_ref.at[0, 0])], v_ref.at[0, 0], GQA_DIM ** -0.5, seq)
    o = acc / l
    for p in range(n_pairs):
        even = o[(2 * p) * tq:(2 * p + 1) * tq]
        odd = o[(2 * p + 1) * tq:(2 * p + 2) * tq]
        o_ref[0, :, p * LANES:(p + 1) * LANES] = jnp.where(_lo_mask(even.shape), even, odd).astype(BF16)


def _gqa_attn_call(q, k, v):
    bsz, n_qblk, seq, _ = q.shape
    tq = Q_TILE
    per_kv = n_qblk // GQA_KV_HEADS
    return pl.pallas_call(
        functools.partial(_gqa_attn_kernel, seq=seq),
        grid=(bsz, GQA_KV_HEADS, seq // tq),
        in_specs=[pl.BlockSpec((1, per_kv, tq, LANES), lambda b, j, i: (b, j, i, 0)),
                  pl.BlockSpec((1, 1, seq, LANES), lambda b, j, i: (b, j, 0, 0)),
                  pl.BlockSpec((1, 1, seq, LANES), lambda b, j, i: (b, j, 0, 0))],
        out_specs=pl.BlockSpec((1, tq, per_kv * LANES), lambda b, j, i: (b, i, j)),
        out_shape=jax.ShapeDtypeStruct((bsz, seq, D_MODEL // 2), BF16),
        compiler_params=_params(3),
        name="gqa_attn",
    )(q, k, v)


def _diff_attn_kernel(q_ref, k_ref, v_ref, lq1_ref, lk1_ref, lq2_ref, lk2_ref, sg_ref, o_ref, *, seq, lam_init):
    tq = q_ref.shape[2]
    qp = q_ref[0, 0]
    lo = _lo_mask(qp.shape)
    q_all = jnp.concatenate([jnp.where(lo, qp, 0), jnp.where(lo, 0, qp)], axis=0).astype(BF16)
    ((acc, l),) = _flash([(q_all, k_ref.at[0, 0])], v_ref.at[0], DIFF_DIM ** -0.5, seq)
    lam = (jnp.exp(jnp.sum(lq1_ref[...] * lk1_ref[...], axis=-1, keepdims=True))
           - jnp.exp(jnp.sum(lq2_ref[...] * lk2_ref[...], axis=-1, keepdims=True)) + lam_init)
    o = acc / l
    od = o[:tq] - lam * o[tq:]
    o_ref[0] = ((_rms(od) * sg_ref[...]) * (1.0 - lam_init)).astype(BF16)


def _diff_attn_call(q, k, v, lq1, lk1, lq2, lk2, subln_g, lam_init):
    bsz, n_heads, seq, _ = q.shape
    tq = Q_TILE
    blk = pl.BlockSpec((1, 1, tq, LANES), lambda b, h, i: (b, h, i, 0))
    return pl.pallas_call(
        functools.partial(_diff_attn_kernel, seq=seq, lam_init=lam_init),
        grid=(bsz, n_heads, seq // tq),
        in_specs=[blk,
                  pl.BlockSpec((1, 1, seq, LANES), lambda b, h, i: (b, h, 0, 0)),
                  pl.BlockSpec((1, seq, LANES), lambda b, h, i: (b, 0, h)),
                  _const_spec(lq1.shape), _const_spec(lk1.shape), _const_spec(lq2.shape), _const_spec(lk2.shape),
                  _const_spec(subln_g.shape)],
        out_specs=pl.BlockSpec((1, tq, LANES), lambda b, h, i: (b, i, h)),
        out_shape=jax.ShapeDtypeStruct((bsz, seq, D_MODEL // 2), BF16),
        compiler_params=_params(3),
        name="diff_attn",
    )(q, k, v, lq1, lk1, lq2, lk2, subln_g)


def _pad_cols(w, n):
    return jnp.pad(w, ((0, 0), (0, n)))


def _prep_even(ev_w_in, mla_w_uq, mla_w_ukv):
    o_kpe = MLA_Q_RANK + MLA_KV_RANK
    wz = jnp.concatenate([
        ev_w_in[:, :o_kpe],
        jnp.zeros((D_MODEL, MLA_NOPE), F32), ev_w_in[:, o_kpe:o_kpe + MLA_ROPE],
        jnp.zeros((D_MODEL, LANES - MLA_NOPE - MLA_ROPE), F32),
        ev_w_in[:, o_kpe + MLA_ROPE:]], axis=1)
    wuq = mla_w_uq.reshape(MLA_Q_RANK, MLA_HEADS, MLA_NOPE + MLA_ROPE)
    wuq = jnp.pad(wuq, ((0, 0), (0, 0), (0, LANES - MLA_NOPE - MLA_ROPE))).reshape(MLA_Q_RANK, MLA_HEADS * LANES)
    wukv = mla_w_ukv.reshape(MLA_KV_RANK, MLA_HEADS, MLA_NOPE + MLA_V)
    wuk = jnp.pad(wukv[:, :, :MLA_NOPE], ((0, 0), (0, 0), (0, LANES - MLA_NOPE)))
    wkv = jnp.concatenate([wuk.reshape(MLA_KV_RANK, MLA_HEADS * LANES),
                           wukv[:, :, MLA_NOPE:].reshape(MLA_KV_RANK, MLA_HEADS * MLA_V)], axis=1)
    return wz.astype(BF16), wuq.astype(BF16), wkv.astype(BF16)


def _prep_odd(od_w_in):
    n_qc = GQA_HEADS * GQA_DIM
    n_kv = GQA_KV_HEADS * GQA_DIM

    def dup(w):
        w = w.reshape(D_MODEL, GQA_KV_HEADS, 1, GQA_DIM)
        return jnp.broadcast_to(w, (D_MODEL, GQA_KV_HEADS, 2, GQA_DIM)).reshape(D_MODEL, 2 * n_kv)

    wz = jnp.concatenate([od_w_in[:, :n_qc], dup(od_w_in[:, n_qc:n_qc + n_kv]),
                          dup(od_w_in[:, n_qc + n_kv:n_qc + 2 * n_kv]), od_w_in[:, n_qc + 2 * n_kv:]], axis=1)
    return wz.astype(BF16)


def _rope_tables(seq):
    pos = jnp.arange(seq)

    def angles(p, n_freq, theta):
        inv = theta ** (-jnp.arange(0, n_freq, dtype=F32) * 2.0 / (2 * n_freq))
        ang = p.astype(F32)[:, None] * inv[None, :]
        return jnp.cos(ang), jnp.sin(ang)

    one = lambda n: jnp.ones((seq, n), F32)
    zero = lambda n: jnp.zeros((seq, n), F32)

    def stack(c, s1, s2):
        reps = LANES // c.shape[1]
        return jnp.stack([jnp.tile(t, (1, reps)) for t in (c, s1, s2)])

    cos, sin = angles(pos, MLA_ROPE // 2, ROPE_THETA)
    pad = LANES - MLA_NOPE - MLA_ROPE
    mla = stack(jnp.concatenate([one(MLA_NOPE), cos, cos, one(pad)], 1),
                jnp.concatenate([zero(MLA_NOPE), -sin, zero(MLA_ROPE // 2), zero(pad)], 1),
                jnp.concatenate([zero(MLA_NOPE), zero(MLA_ROPE // 2), sin, zero(pad)], 1))
    q = GQA_DIM // 4
    cr, sr = angles(pos // GRID_W, q, AXIAL_THETA)
    cc, sc = angles(pos % GRID_W, q, AXIAL_THETA)
    axial = stack(jnp.concatenate([cr, cr, cc, cc], 1),
                  jnp.concatenate([-sr, zero(q), -sc, zero(q)], 1),
                  jnp.concatenate([zero(q), sr, zero(q), sc], 1))
    hr = DIFF_ROPE // 2
    cd, sd = angles(pos, hr, ROPE_THETA)
    rest = DIFF_DIM - DIFF_ROPE
    diff = stack(jnp.concatenate([cd, cd, one(rest)], 1),
                 jnp.concatenate([-sd, zero(hr), zero(rest)], 1),
                 jnp.concatenate([zero(hr), sd, zero(rest)], 1))
    return mla, axial, diff


def kernel(x_prompt, x_sample, c_prompt, c_sample, w_ada, b_ada, norm_g, ffn_w_in, ffn_w_out, ev_w_in, mla_gq, mla_w_uq, mla_gkv, mla_w_ukv, pool_w, pool_scale, ev_w_out, od_w_in, gqa_gq, gqa_gk, diff_lq1, diff_lk1, diff_lq2, diff_lk2, diff_subln_g, od_w_out, final_g):
    groups = [x_prompt, x_sample]
    n_rows = [x.shape[0] for x in groups]
    c_all = jnp.concatenate([c_prompt, c_sample], axis=0)
    c_all = jnp.pad(c_all, ((0, -c_all.shape[0] % 16), (0, 0)))
    mod = _ada_call(c_all, w_ada, b_ada)
    mods, r0 = [], 0
    for n in n_rows:
        mods.append(mod[:, r0:r0 + n].reshape(DEPTH, n, 1, 9 * D_MODEL))
        r0 += n

    ffn_in = ffn_w_in.astype(BF16)
    ffn_out = ffn_w_out.astype(BF16)
    row = lambda v: v.reshape(1, -1)
    tables = [_rope_tables(x.shape[1]) for x in groups]

    xs = list(groups)
    for i in range(DEPTH):
        j = i // 2
        if i % 2 == 0:
            wz, wuq, wkv = _prep_even(ev_w_in[j], mla_w_uq[j], mla_w_ukv[j])
            w_mix = ev_w_out[j].astype(BF16)
            poolw = pool_w[j].astype(BF16)
        else:
            wz = _prep_odd(od_w_in[j])
            w_mix = od_w_out[j].astype(BF16)
            gq2 = row(jnp.tile(gqa_gq[j], 2))
            gk2 = row(jnp.tile(gqa_gk[j], 2))
            lam_init = 0.8 - 0.6 * math.exp(-0.3 * i)
        for gi in range(len(groups)):
            x, md = xs[gi], mods[gi]
            tab_mla, tab_axial, tab_diff = tables[gi]
            x = _ffn_call(x, md, i, 0, row(norm_g[i, 0]), ffn_in[i, 0], ffn_out[i, 0])
            if i % 2 == 0:
                q, k, v, ob = _even_in_call(x, md, i, row(norm_g[i, 1]), wz, row(mla_gq[j]), wuq, row(mla_gkv[j]),
                                            wkv, tab_mla, poolw, row(pool_scale[j]))
                mix = (_mla_attn_call(q, k, v), ob, w_mix)
            else:
                qc, kc, vc, qd, kd, vd = _odd_in_call(x, md, i, row(norm_g[i, 1]), wz, gq2, gk2, tab_axial, tab_diff)
                oc = _gqa_attn_call(qc, kc, vc)
                od = _diff_attn_call(qd, kd, vd, row(diff_lq1[j]), row(diff_lk1[j]), row(diff_lq2[j]),
                                     row(diff_lk2[j]), row(diff_subln_g[j]), lam_init)
                mix = (oc, od, w_mix)
            xs[gi] = _ffn_call(x, md, i, 1, row(norm_g[i, 2]), ffn_in[i, 1], ffn_out[i, 1], mix=mix,
                               final_g=row(final_g) if i == DEPTH - 1 else None)
    return tuple(xs)
```

```python
import functools
import math

import jax
import jax.numpy as jnp
from jax import lax
from jax.experimental import pallas as pl
from jax.experimental.pallas import tpu as pltpu

F32 = jnp.float32
BF16 = jnp.bfloat16

D_MODEL = 1024
DEPTH = 4
GRID_W = 64
EPS = 1e-6
D_FF = 2816
ROPE_THETA = 500000.0
AXIAL_THETA = 10000.0
MLA_HEADS = 8
MLA_NOPE = 64
MLA_ROPE = 32
MLA_V = 64
MLA_Q_RANK = 384
MLA_KV_RANK = 256
POOL_WINDOWS = (2, 4, 8, 16)
POOL_GROUP = 128
POOL_WIDTH = POOL_GROUP * len(POOL_WINDOWS)
POOL_HALO = 8
GQA_HEADS = 8
GQA_KV_HEADS = 2
GQA_DIM = 64
DIFF_HEADS = 4
DIFF_DIM = 64
DIFF_ROPE = DIFF_DIM // 4
LANES = 128
LOG2E = math.log2(math.e)

TOKEN_TILE = 512
KV_CHUNK = 512
SCORE_LANES = 2048
EXP_ROWS = 16
ADA_COL_TILE = 1536
FFN_CHUNKS = ((0, 1024), (1024, 1024), (2048, 768))
VMEM_LIMIT = 56 * 1024 * 1024


def _params(n_grid):
    return pltpu.CompilerParams(dimension_semantics=("arbitrary",) * n_grid, vmem_limit_bytes=VMEM_LIMIT)


def _const_spec(shape):
    nd = len(shape)
    return pl.BlockSpec(shape, lambda *_: (0,) * nd, pipeline_mode=pl.Buffered(1))


def _mod_spec(layer, chunk):
    return pl.BlockSpec((None, None, 1, D_MODEL), lambda b, *_: (layer, b, 0, chunk))


def _dot(a, b):
    return jnp.dot(a, b, preferred_element_type=F32)


def _rms(x):
    return x * lax.rsqrt(jnp.mean(x * x, axis=-1, keepdims=True) + EPS)


def _modnorm(x, g, scale, shift):
    return (_rms(x) * g) * (1.0 + scale) + shift


def _silu(x):
    return x * jax.nn.sigmoid(x)


def _rope(x, tab_ref, half):
    width = x.shape[-1]
    fwd = pltpu.roll(x, width - half, axis=1)
    back = pltpu.roll(x, half, axis=1)
    return x * tab_ref[0] + fwd * tab_ref[1] + back * tab_ref[2]


def _store_value_chunks(vt_ref, v):
    for blk in range(v.shape[1] // LANES):
        vt = v[:, blk * LANES:(blk + 1) * LANES].T.astype(BF16)
        for ci in range(v.shape[0] // KV_CHUNK):
            vt_ref[0, blk, ci] = vt[:, ci * KV_CHUNK:(ci + 1) * KV_CHUNK]


def _value_chunk_out(bsz, seq, n_blk):
    per_tile = TOKEN_TILE // KV_CHUNK
    spec = pl.BlockSpec((1, n_blk, per_tile, LANES, KV_CHUNK), lambda b, t: (b, 0, t, 0, 0))
    return spec, jax.ShapeDtypeStruct((bsz, n_blk, seq // KV_CHUNK, LANES, KV_CHUNK), BF16)


def _ada_kernel(c_ref, w_ref, b_ref, o_ref):
    sc = _silu(c_ref[...]).astype(BF16)
    o_ref[0] = _dot(sc, w_ref[0].astype(BF16)) + b_ref[0]


def _ada_call(c_all, w_ada, b_ada):
    rows = c_all.shape[0]
    n_out = w_ada.shape[-1]
    return pl.pallas_call(
        _ada_kernel,
        grid=(DEPTH, n_out // ADA_COL_TILE),
        in_specs=[
            pl.BlockSpec((rows, D_MODEL), lambda i, n: (0, 0)),
            pl.BlockSpec((1, D_MODEL, ADA_COL_TILE), lambda i, n: (i, 0, n)),
            pl.BlockSpec((1, 1, ADA_COL_TILE), lambda i, n: (i, 0, n)),
        ],
        out_specs=pl.BlockSpec((1, rows, ADA_COL_TILE), lambda i, n: (i, 0, n)),
        out_shape=jax.ShapeDtypeStruct((DEPTH, rows, n_out), F32),
        compiler_params=_params(2),
        name="ada_mod",
    )(c_all, w_ada, b_ada.reshape(DEPTH, 1, n_out))


def _ffn_kernel(*refs, has_mix, has_final):
    it = iter(refs)
    x_ref = next(it)
    if has_mix:
        ma_ref, mb_ref, wmix_ref, g1_ref = next(it), next(it), next(it), next(it)
    sh_ref, sc_ref, gt_ref, ng_ref, win_ref, wout_ref = (next(it) for _ in range(6))
    if has_final:
        fg_ref = next(it)
    o_ref = next(it)

    x = x_ref[0]
    if has_mix:
        half = wmix_ref.shape[0] // 2
        m = _dot(ma_ref[0], wmix_ref[:half, :]) + _dot(mb_ref[0], wmix_ref[half:, :])
        x = x + g1_ref[...] * m
    h = _modnorm(x, ng_ref[...], sc_ref[...], sh_ref[...]).astype(BF16)
    acc = None
    for c0, cn in FFN_CHUNKS:
        g = _dot(h, win_ref[:, c0:c0 + cn])
        u = _dot(h, win_ref[:, D_FF + c0:D_FF + c0 + cn])
        a = (_silu(g) * u).astype(BF16)
        d = _dot(a, wout_ref[c0:c0 + cn, :])
        acc = d if acc is None else acc + d
    y = x + (0.5 * gt_ref[...]) * acc
    if has_final:
        y = _rms(y) * fg_ref[...]
    o_ref[0] = y


def _ffn_call(x, mod, layer, which, norm_g_row, w_in, w_out, mix=None, final_g=None):
    bsz, seq, _ = x.shape
    tm = TOKEN_TILE
    base = 0 if which == 0 else 6
    tok = pl.BlockSpec((1, tm, D_MODEL), lambda b, t: (b, t, 0))
    args, specs = [x], [tok]
    if mix is not None:
        ma, mb, wmix = mix
        half = pl.BlockSpec((1, tm, D_MODEL // 2), lambda b, t: (b, t, 0))
        args += [ma, mb, wmix, mod]
        specs += [half, half, _const_spec(wmix.shape), _mod_spec(layer, 5)]
    args += [mod, mod, mod, norm_g_row, w_in, w_out]
    specs += [_mod_spec(layer, base), _mod_spec(layer, base + 1), _mod_spec(layer, base + 2),
              _const_spec(norm_g_row.shape), _const_spec(w_in.shape), _const_spec(w_out.shape)]
    if final_g is not None:
        args.append(final_g)
        specs.append(_const_spec(final_g.shape))
    return pl.pallas_call(
        functools.partial(_ffn_kernel, has_mix=mix is not None, has_final=final_g is not None),
        grid=(bsz, seq // tm),
        in_specs=specs,
        out_specs=tok,
        out_shape=jax.ShapeDtypeStruct(x.shape, F32),
        compiler_params=_params(2),
        name="ffn",
    )(*args)


def _pool(pz, halo, t_idx, n_tiles, seq, poolw_ref, pscale_ref):
    tm = pz.shape[0]
    prev = jnp.where(t_idx > 0, halo[:POOL_HALO], 0.0)
    nxt = jnp.where(t_idx < n_tiles - 1, halo[POOL_HALO:], 0.0)
    ext = jnp.concatenate([prev, pz, nxt], axis=0)
    n_ext = ext.shape[0]
    pos = t_idx * tm + lax.broadcasted_iota(jnp.int32, (tm, 1), 0)
    outs = []
    for g, w in enumerate(POOL_WINDOWS):
        sl = slice(g * POOL_GROUP, (g + 1) * POOL_GROUP)
        run = ext[:, sl]
        span = 1
        while span < w:
            run = run + pltpu.roll(run, n_ext - span, axis=0)
            span *= 2
        start = POOL_HALO - w // 2
        if start:
            run = pltpu.roll(run, n_ext - start, axis=0)
        win = run[:tm]
        cnt = jnp.minimum(pos + (w - w // 2), seq) - jnp.maximum(pos - w // 2, 0)
        d = (win / cnt.astype(F32) - pz[:, sl]).astype(BF16)
        outs.append(_dot(d, poolw_ref[g]))
    return jnp.concatenate(outs, axis=-1) * pscale_ref[...]


def _even_in_kernel(x_ref, xp_ref, xn_ref, sh_ref, sc_ref, ng_ref, wz_ref, gq_ref, wuq_ref, gkv_ref, wkv_ref,
                    tab_ref, poolw_ref, pscale_ref, q_ref, k_ref, v_ref, ob_ref, *, seq):
    t_idx = pl.program_id(1)
    n_tiles = pl.num_programs(1)
    g, sc, sh = ng_ref[...], sc_ref[...], sh_ref[...]
    h = _modnorm(x_ref[0], g, sc, sh).astype(BF16)
    z = _dot(h, wz_ref[...])
    o_kpe = MLA_Q_RANK + MLA_KV_RANK
    o_pz = o_kpe + LANES

    xh = jnp.concatenate([xp_ref[0], xn_ref[0]], axis=0)
    hh = _modnorm(xh, g, sc, sh).astype(BF16)
    halo = _dot(hh, wz_ref[:, o_pz:])
    ob_ref[0] = _pool(z[:, o_pz:], halo, t_idx, n_tiles, seq, poolw_ref, pscale_ref).astype(BF16)

    cq = (_rms(z[:, :MLA_Q_RANK]) * gq_ref[...]).astype(BF16)
    ckv = (_rms(z[:, MLA_Q_RANK:o_kpe]) * gkv_ref[...]).astype(BF16)
    q_all = _dot(cq, wuq_ref[...])
    kv_all = _dot(ckv, wkv_ref[...])
    kpe = _rope(z[:, o_kpe:o_pz], tab_ref, MLA_ROPE // 2)
    for hd in range(MLA_HEADS):
        sl = slice(hd * LANES, (hd + 1) * LANES)
        q_ref[0, hd] = _rope(q_all[:, sl], tab_ref, MLA_ROPE // 2).T.astype(BF16)
        k_ref[0, hd] = (kv_all[:, sl] + kpe).astype(BF16)
    _store_value_chunks(v_ref, kv_all[:, MLA_HEADS * LANES:])


def _even_in_call(x, mod, layer, norm_g_row, wz, gq, wuq, gkv, wkv, tab, poolw, pscale):
    bsz, seq, _ = x.shape
    tm = TOKEN_TILE
    hb = tm // POOL_HALO
    n_hblk = seq // POOL_HALO
    tok = pl.BlockSpec((1, tm, D_MODEL), lambda b, t: (b, t, 0))
    prev = pl.BlockSpec((1, POOL_HALO, D_MODEL), lambda b, t: (b, jnp.maximum(t * hb - 1, 0), 0))
    nxt = pl.BlockSpec((1, POOL_HALO, D_MODEL), lambda b, t: (b, jnp.minimum((t + 1) * hb, n_hblk - 1), 0))
    heads = pl.BlockSpec((1, MLA_HEADS, tm, LANES), lambda b, t: (b, 0, t, 0))
    heads_t = pl.BlockSpec((1, MLA_HEADS, LANES, tm), lambda b, t: (b, 0, 0, t))
    half = pl.BlockSpec((1, tm, D_MODEL // 2), lambda b, t: (b, t, 0))
    out_heads = jax.ShapeDtypeStruct((bsz, MLA_HEADS, seq, LANES), BF16)
    out_heads_t = jax.ShapeDtypeStruct((bsz, MLA_HEADS, LANES, seq), BF16)
    out_half = jax.ShapeDtypeStruct((bsz, seq, D_MODEL // 2), BF16)
    vt_spec, vt_shape = _value_chunk_out(bsz, seq, MLA_HEADS * MLA_V // LANES)
    return pl.pallas_call(
        functools.partial(_even_in_kernel, seq=seq),
        grid=(bsz, seq // tm),
        in_specs=[tok, prev, nxt, _mod_spec(layer, 3), _mod_spec(layer, 4), _const_spec(norm_g_row.shape),
                  _const_spec(wz.shape), _const_spec(gq.shape), _const_spec(wuq.shape), _const_spec(gkv.shape),
                  _const_spec(wkv.shape), pl.BlockSpec((3, tm, LANES), lambda b, t: (0, t, 0)),
                  _const_spec(poolw.shape), _const_spec(pscale.shape)],
        out_specs=[heads_t, heads, vt_spec, half],
        out_shape=[out_heads_t, out_heads, vt_shape, out_half],
        compiler_params=_params(2),
        name="even_in",
    )(x, x, x, mod, mod, norm_g_row, wz, gq, wuq, gkv, wkv, tab, poolw, pscale)


def _pair_norm(x, gain):
    lo_mask = lax.broadcasted_iota(jnp.int32, x.shape, 1) < GQA_DIM
    sq = x * x
    lo = jnp.sum(jnp.where(lo_mask, sq, 0.0), axis=-1, keepdims=True)
    hi = jnp.sum(jnp.where(lo_mask, 0.0, sq), axis=-1, keepdims=True)
    inv = jnp.where(lo_mask, lax.rsqrt(lo / GQA_DIM + EPS), lax.rsqrt(hi / GQA_DIM + EPS))
    return x * inv * gain


def _odd_in_kernel(x_ref, sh_ref, sc_ref, ng_ref, wz_ref, gq_ref, gk_ref, tabc_ref, tabd_ref,
                   qc_ref, kc_ref, vc_ref, qd_ref, kd_ref, vd_ref):
    h = _modnorm(x_ref[0], ng_ref[...], sc_ref[...], sh_ref[...]).astype(BF16)
    z = _dot(h, wz_ref[...])
    n_qc = GQA_HEADS * GQA_DIM // LANES
    n_kv = GQA_KV_HEADS
    n_d = DIFF_HEADS
    col = 0
    for p in range(n_qc):
        blk = z[:, col:col + LANES]
        qc_ref[0, p] = _rope(_pair_norm(blk, gq_ref[...]), tabc_ref, GQA_DIM // 4).T.astype(BF16)
        col += LANES
    for j in range(n_kv):
        blk = z[:, col:col + LANES]
        kc_ref[0, j] = _rope(_pair_norm(blk, gk_ref[...]), tabc_ref, GQA_DIM // 4).astype(BF16)
        col += LANES
    _store_value_chunks(vc_ref, z[:, col:col + n_kv * LANES])
    col += n_kv * LANES
    for hd in range(n_d):
        qd_ref[0, hd] = _rope(z[:, col:col + LANES], tabd_ref, DIFF_ROPE // 2).T.astype(BF16)
        col += LANES
    for hd in range(n_d):
        kd_ref[0, hd] = _rope(z[:, col:col + LANES], tabd_ref, DIFF_ROPE // 2).astype(BF16)
        col += LANES
    _store_value_chunks(vd_ref, z[:, col:])


def _odd_in_call(x, mod, layer, norm_g_row, wz, gq2, gk2, tabc, tabd):
    bsz, seq, _ = x.shape
    tm = TOKEN_TILE
    tok = pl.BlockSpec((1, tm, D_MODEL), lambda b, t: (b, t, 0))
    tab = pl.BlockSpec((3, tm, LANES), lambda b, t: (0, t, 0))

    def blocks(n):
        return (pl.BlockSpec((1, n, tm, LANES), lambda b, t: (b, 0, t, 0)),
                jax.ShapeDtypeStruct((bsz, n, seq, LANES), BF16))

    def blocks_t(n):
        return (pl.BlockSpec((1, n, LANES, tm), lambda b, t: (b, 0, 0, t)),
                jax.ShapeDtypeStruct((bsz, n, LANES, seq), BF16))

    n_qc = GQA_HEADS * GQA_DIM // LANES
    outs = [blocks_t(n_qc), blocks(GQA_KV_HEADS), _value_chunk_out(bsz, seq, GQA_KV_HEADS),
            blocks_t(DIFF_HEADS), blocks(DIFF_HEADS), _value_chunk_out(bsz, seq, DIFF_HEADS)]
    return pl.pallas_call(
        _odd_in_kernel,
        grid=(bsz, seq // tm),
        in_specs=[tok, _mod_spec(layer, 3), _mod_spec(layer, 4), _const_spec(norm_g_row.shape),
                  _const_spec(wz.shape), _const_spec(gq2.shape), _const_spec(gk2.shape), tab, tab],
        out_specs=[o[0] for o in outs],
        out_shape=[o[1] for o in outs],
        compiler_params=_params(2),
        name="odd_in",
    )(x, mod, mod, norm_g_row, wz, gq2, gk2, tabc, tabd)


def _flash_t(streams, vt_ref, scale, n_chunks, s_bufs, p_bufs, acc_ref):
    c = scale * LOG2E
    tk, n = s_bufs[0].shape
    assert n_chunks % 2 == 0 and n_chunks >= 2
    offs, o = [], 0
    for qt, _ in streams:
        offs.append(o)
        o += qt.shape[1]
    assert o == n

    def scores(j, s_ref):
        off = pl.multiple_of(j * tk, tk)
        cms = []
        for (qt, k_ref), o in zip(streams, offs):
            s = _dot(k_ref[pl.ds(off, tk), :], qt)
            s_ref[:, o:o + qt.shape[1]] = s
            cms.append(jnp.max(s, axis=0, keepdims=True))
        return jnp.concatenate(cms, axis=1) if len(cms) > 1 else cms[0]

    def step(j, carry, cur, other, do_scores=True, do_values=True):
        m, l, cm, alpha_prev = carry
        m_new = jnp.maximum(m, cm)
        alpha = jnp.exp2((m - m_new) * c)
        mc = m_new * c
        cm_next = scores(j + 1, s_bufs[other]) if do_scores else cm
        if do_values:
            acc_ref[...] = alpha_prev * acc_ref[...] + _dot(vt_ref[j - 1], p_bufs[other][...])
        lsum = jnp.zeros((8, n), F32)
        for r in range(0, tk, EXP_ROWS):
            p = jnp.exp2(s_bufs[cur][r:r + EXP_ROWS, :] * c - mc)
            for q in range(0, EXP_ROWS, 8):
                lsum = lsum + p[q:q + 8]
            p_bufs[cur][r:r + EXP_ROWS, :] = p.astype(BF16)
        l_new = alpha * l + jnp.sum(lsum, axis=0, keepdims=True)
        return m_new, l_new, cm_next, alpha

    acc_ref[...] = jnp.zeros(acc_ref.shape, F32)
    cm0 = scores(0, s_bufs[0])
    carry = (jnp.full((1, n), -jnp.inf, F32), jnp.zeros((1, n), F32), cm0, jnp.ones((1, n), F32))
    carry = step(0, carry, 0, 1, do_values=False)

    def body(jj, carry):
        carry = step(2 * jj + 1, carry, 1, 0)
        return step(2 * jj + 2, carry, 0, 1)

    carry = lax.fori_loop(0, (n_chunks - 2) // 2, body, carry)
    _, l, _, alpha = step(n_chunks - 1, carry, 1, 0, do_scores=False)
    acc = alpha * acc_ref[...] + _dot(vt_ref[n_chunks - 1], p_bufs[1][...])
    return acc * (1.0 / l)


def _row_lo(shape):
    return lax.broadcasted_iota(jnp.int32, shape, 0) < LANES // 2


def _attn_scratch(n):
    return [[pltpu.VMEM((KV_CHUNK, n), F32)] * 2, [pltpu.VMEM((KV_CHUNK, n), BF16)] * 2, pltpu.VMEM((LANES, n), F32)]


def _mla_attn_kernel(qt_ref, k_ref, vt_ref, o_ref, s_bufs, p_bufs, acc_ref):
    tq = qt_ref.shape[3]
    o = _flash_t([(qt_ref[0, 0], k_ref.at[0, 0]), (qt_ref[0, 1], k_ref.at[0, 1])], vt_ref.at[0, 0],
                 (MLA_NOPE + MLA_ROPE) ** -0.5, vt_ref.shape[2], s_bufs, p_bufs, acc_ref)
    o_ref[0] = jnp.where(_row_lo((LANES, tq)), o[:, :tq], o[:, tq:]).T.astype(BF16)


def _mla_attn_call(qt, k, vt):
    bsz, _, _, seq = qt.shape
    n_chunks = vt.shape[2]
    tq = min(SCORE_LANES // 2, seq)
    return pl.pallas_call(
        _mla_attn_kernel,
        grid=(bsz, MLA_HEADS // 2, seq // tq),
        in_specs=[pl.BlockSpec((1, 2, LANES, tq), lambda b, p, i: (b, p, 0, i)),
                  pl.BlockSpec((1, 2, seq, LANES), lambda b, p, i: (b, p, 0, 0)),
                  pl.BlockSpec((1, 1, n_chunks, LANES, KV_CHUNK), lambda b, p, i: (b, p, 0, 0, 0))],
        out_specs=pl.BlockSpec((1, tq, LANES), lambda b, p, i: (b, i, p)),
        out_shape=jax.ShapeDtypeStruct((bsz, seq, D_MODEL // 2), BF16),
        scratch_shapes=_attn_scratch(2 * tq),
        compiler_params=_params(3),
        name="mla_attn",
    )(qt, k, vt)


def _split_halves_t(qt_ref):
    parts = []
    for p in range(qt_ref.shape[1]):
        qp = qt_ref[0, p]
        lo = _row_lo(qp.shape)
        parts += [jnp.where(lo, qp, 0), jnp.where(lo, 0, qp)]
    return jnp.concatenate(parts, axis=1).astype(BF16)


def _gqa_attn_kernel(qt_ref, k_ref, vt_ref, o_ref, s_bufs, p_bufs, acc_ref):
    n_pairs, tq = qt_ref.shape[1], qt_ref.shape[3]
    o = _flash_t([(_split_halves_t(qt_ref), k_ref.at[0, 0])], vt_ref.at[0, 0], GQA_DIM ** -0.5, vt_ref.shape[2],
                 s_bufs, p_bufs, acc_ref)
    for p in range(n_pairs):
        even = o[:, (2 * p) * tq:(2 * p + 1) * tq]
        odd = o[:, (2 * p + 1) * tq:(2 * p + 2) * tq]
        o_ref[0, :, p * LANES:(p + 1) * LANES] = jnp.where(_row_lo(even.shape), even, odd).T.astype(BF16)


def _gqa_attn_call(qt, k, vt):
    bsz, n_qblk, _, seq = qt.shape
    n_chunks = vt.shape[2]
    per_kv = n_qblk // GQA_KV_HEADS
    tq = min(SCORE_LANES // (2 * per_kv), seq)
    return pl.pallas_call(
        _gqa_attn_kernel,
        grid=(bsz, GQA_KV_HEADS, seq // tq),
        in_specs=[pl.BlockSpec((1, per_kv, LANES, tq), lambda b, j, i: (b, j, 0, i)),
                  pl.BlockSpec((1, 1, seq, LANES), lambda b, j, i: (b, j, 0, 0)),
                  pl.BlockSpec((1, 1, n_chunks, LANES, KV_CHUNK), lambda b, j, i: (b, j, 0, 0, 0))],
        out_specs=pl.BlockSpec((1, tq, per_kv * LANES), lambda b, j, i: (b, i, j)),
        out_shape=jax.ShapeDtypeStruct((bsz, seq, D_MODEL // 2), BF16),
        scratch_shapes=_attn_scratch(2 * per_kv * tq),
        compiler_params=_params(3),
        name="gqa_attn",
    )(qt, k, vt)


def _diff_attn_kernel(qt_ref, k_ref, vt_ref, lq1_ref, lk1_ref, lq2_ref, lk2_ref, sg_ref, o_ref, s_bufs, p_bufs,
                      acc_ref, *, lam_init):
    tq = qt_ref.shape[3]
    o = _flash_t([(_split_halves_t(qt_ref), k_ref.at[0, 0])], vt_ref.at[0, 0], DIFF_DIM ** -0.5, vt_ref.shape[2],
                 s_bufs, p_bufs, acc_ref)
    lam = (jnp.exp(jnp.sum(lq1_ref[...] * lk1_ref[...], axis=-1, keepdims=True))
           - jnp.exp(jnp.sum(lq2_ref[...] * lk2_ref[...], axis=-1, keepdims=True)) + lam_init)
    od = o[:, :tq] - lam * o[:, tq:]
    od = od * lax.rsqrt(jnp.mean(od * od, axis=0, keepdims=True) + EPS)
    o_ref[0] = ((od * sg_ref[...]) * (1.0 - lam_init)).T.astype(BF16)


def _diff_attn_call(qt, k, vt, lq1, lk1, lq2, lk2, subln_g_col, lam_init):
    bsz, n_heads, _, seq = qt.shape
    n_chunks = vt.shape[2]
    tq = min(SCORE_LANES // 2, seq)
    return pl.pallas_call(
        functools.partial(_diff_attn_kernel, lam_init=lam_init),
        grid=(bsz, n_heads, seq // tq),
        in_specs=[pl.BlockSpec((1, 1, LANES, tq), lambda b, h, i: (b, h, 0, i)),
                  pl.BlockSpec((1, 1, seq, LANES), lambda b, h, i: (b, h, 0, 0)),
                  pl.BlockSpec((1, 1, n_chunks, LANES, KV_CHUNK), lambda b, h, i: (b, h, 0, 0, 0)),
                  _const_spec(lq1.shape), _const_spec(lk1.shape), _const_spec(lq2.shape), _const_spec(lk2.shape),
                  _const_spec(subln_g_col.shape)],
        out_specs=pl.BlockSpec((1, tq, LANES), lambda b, h, i: (b, i, h)),
        out_shape=jax.ShapeDtypeStruct((bsz, seq, D_MODEL // 2), BF16),
        scratch_shapes=_attn_scratch(2 * tq),
        compiler_params=_params(3),
        name="diff_attn",
    )(qt, k, vt, lq1, lk1, lq2, lk2, subln_g_col)


def _pad_cols(w, n):
    return jnp.pad(w, ((0, 0), (0, n)))


def _prep_even(ev_w_in, mla_w_uq, mla_w_ukv):
    o_kpe = MLA_Q_RANK + MLA_KV_RANK
    wz = jnp.concatenate([
        ev_w_in[:, :o_kpe],
        jnp.zeros((D_MODEL, MLA_NOPE), F32), ev_w_in[:, o_kpe:o_kpe + MLA_ROPE],
        jnp.zeros((D_MODEL, LANES - MLA_NOPE - MLA_ROPE), F32),
        ev_w_in[:, o_kpe + MLA_ROPE:]], axis=1)
    wuq = mla_w_uq.reshape(MLA_Q_RANK, MLA_HEADS, MLA_NOPE + MLA_ROPE)
    wuq = jnp.pad(wuq, ((0, 0), (0, 0), (0, LANES - MLA_NOPE - MLA_ROPE))).reshape(MLA_Q_RANK, MLA_HEADS * LANES)
    wukv = mla_w_ukv.reshape(MLA_KV_RANK, MLA_HEADS, MLA_NOPE + MLA_V)
    wuk = jnp.pad(wukv[:, :, :MLA_NOPE], ((0, 0), (0, 0), (0, LANES - MLA_NOPE)))
    wkv = jnp.concatenate([wuk.reshape(MLA_KV_RANK, MLA_HEADS * LANES),
                           wukv[:, :, MLA_NOPE:].reshape(MLA_KV_RANK, MLA_HEADS * MLA_V)], axis=1)
    return wz.astype(BF16), wuq.astype(BF16), wkv.astype(BF16)


def _prep_odd(od_w_in):
    n_qc = GQA_HEADS * GQA_DIM
    n_kv = GQA_KV_HEADS * GQA_DIM

    def dup(w):
        w = w.reshape(D_MODEL, GQA_KV_HEADS, 1, GQA_DIM)
        return jnp.broadcast_to(w, (D_MODEL, GQA_KV_HEADS, 2, GQA_DIM)).reshape(D_MODEL, 2 * n_kv)

    wz = jnp.concatenate([od_w_in[:, :n_qc], dup(od_w_in[:, n_qc:n_qc + n_kv]),
                          dup(od_w_in[:, n_qc + n_kv:n_qc + 2 * n_kv]), od_w_in[:, n_qc + 2 * n_kv:]], axis=1)
    return wz.astype(BF16)


def _rope_tables(seq):
    pos = jnp.arange(seq)

    def angles(p, n_freq, theta):
        inv = theta ** (-jnp.arange(0, n_freq, dtype=F32) * 2.0 / (2 * n_freq))
        ang = p.astype(F32)[:, None] * inv[None, :]
        return jnp.cos(ang), jnp.sin(ang)

    one = lambda n: jnp.ones((seq, n), F32)
    zero = lambda n: jnp.zeros((seq, n), F32)

    def stack(c, s1, s2):
        reps = LANES // c.shape[1]
        return jnp.stack([jnp.tile(t, (1, reps)) for t in (c, s1, s2)])

    cos, sin = angles(pos, MLA_ROPE // 2, ROPE_THETA)
    pad = LANES - MLA_NOPE - MLA_ROPE
    mla = stack(jnp.concatenate([one(MLA_NOPE), cos, cos, one(pad)], 1),
                jnp.concatenate([zero(MLA_NOPE), -sin, zero(MLA_ROPE // 2), zero(pad)], 1),
                jnp.concatenate([zero(MLA_NOPE), zero(MLA_ROPE // 2), sin, zero(pad)], 1))
    q = GQA_DIM // 4
    cr, sr = angles(pos // GRID_W, q, AXIAL_THETA)
    cc, sc = angles(pos % GRID_W, q, AXIAL_THETA)
    axial = stack(jnp.concatenate([cr, cr, cc, cc], 1),
                  jnp.concatenate([-sr, zero(q), -sc, zero(q)], 1),
                  jnp.concatenate([zero(q), sr, zero(q), sc], 1))
    hr = DIFF_ROPE // 2
    cd, sd = angles(pos, hr, ROPE_THETA)
    rest = DIFF_DIM - DIFF_ROPE
    diff = stack(jnp.concatenate([cd, cd, one(rest)], 1),
                 jnp.concatenate([-sd, zero(hr), zero(rest)], 1),
                 jnp.concatenate([zero(hr), sd, zero(rest)], 1))
    return mla, axial, diff


def kernel(x_prompt, x_sample, c_prompt, c_sample, w_ada, b_ada, norm_g, ffn_w_in, ffn_w_out, ev_w_in, mla_gq, mla_w_uq, mla_gkv, mla_w_ukv, pool_w, pool_scale, ev_w_out, od_w_in, gqa_gq, gqa_gk, diff_lq1, diff_lk1, diff_lq2, diff_lk2, diff_subln_g, od_w_out, final_g):
    groups = [x_prompt, x_sample]
    n_rows = [x.shape[0] for x in groups]
    c_all = jnp.concatenate([c_prompt, c_sample], axis=0)
    c_all = jnp.pad(c_all, ((0, -c_all.shape[0] % 16), (0, 0)))
    mod = _ada_call(c_all, w_ada, b_ada)
    mods, r0 = [], 0
    for n in n_rows:
        mods.append(mod[:, r0:r0 + n].reshape(DEPTH, n, 1, 9 * D_MODEL))
        r0 += n

    ffn_in = ffn_w_in.astype(BF16)
    ffn_out = ffn_w_out.astype(BF16)
    row = lambda v: v.reshape(1, -1)
    tables = [_rope_tables(x.shape[1]) for x in groups]

    xs = list(groups)
    for i in range(DEPTH):
        j = i // 2
        if i % 2 == 0:
            wz, wuq, wkv = _prep_even(ev_w_in[j], mla_w_uq[j], mla_w_ukv[j])
            w_mix = ev_w_out[j].astype(BF16)
            poolw = pool_w[j].astype(BF16)
        else:
            wz = _prep_odd(od_w_in[j])
            w_mix = od_w_out[j].astype(BF16)
            gq2 = row(jnp.tile(gqa_gq[j], 2))
            gk2 = row(jnp.tile(gqa_gk[j], 2))
            lam_init = 0.8 - 0.6 * math.exp(-0.3 * i)
        for gi in range(len(groups)):
            x, md = xs[gi], mods[gi]
            tab_mla, tab_axial, tab_diff = tables[gi]
            x = _ffn_call(x, md, i, 0, row(norm_g[i, 0]), ffn_in[i, 0], ffn_out[i, 0])
            if i % 2 == 0:
                q, k, v, ob = _even_in_call(x, md, i, row(norm_g[i, 1]), wz, row(mla_gq[j]), wuq, row(mla_gkv[j]),
                                            wkv, tab_mla, poolw, row(pool_scale[j]))
                mix = (_mla_attn_call(q, k, v), ob, w_mix)
            else:
                qc, kc, vc, qd, kd, vd = _odd_in_call(x, md, i, row(norm_g[i, 1]), wz, gq2, gk2, tab_axial, tab_diff)
                oc = _gqa_attn_call(qc, kc, vc)
                od = _diff_attn_call(qd, kd, vd, row(diff_lq1[j]), row(diff_lk1[j]), row(diff_lq2[j]),
                                     row(diff_lk2[j]), diff_subln_g[j].reshape(-1, 1), lam_init)
                mix = (oc, od, w_mix)
            xs[gi] = _ffn_call(x, md, i, 1, row(norm_g[i, 2]), ffn_in[i, 1], ffn_out[i, 1], mix=mix,
                               final_g=row(final_g) if i == DEPTH - 1 else None)
    return tuple(xs)
```

```python
import functools
import math

import jax
import jax.numpy as jnp
from jax import lax
from jax.experimental import pallas as pl
from jax.experimental.pallas import tpu as pltpu

F32 = jnp.float32
BF16 = jnp.bfloat16

D_MODEL = 1024
DEPTH = 4
GRID_W = 64
EPS = 1e-6
D_FF = 2816
ROPE_THETA = 500000.0
AXIAL_THETA = 10000.0
MLA_HEADS = 8
MLA_NOPE = 64
MLA_ROPE = 32
MLA_V = 64
MLA_Q_RANK = 384
MLA_KV_RANK = 256
POOL_WINDOWS = (2, 4, 8, 16)
POOL_GROUP = 128
POOL_WIDTH = POOL_GROUP * len(POOL_WINDOWS)
POOL_HALO = 8
GQA_HEADS = 8
GQA_KV_HEADS = 2
GQA_DIM = 64
DIFF_HEADS = 4
DIFF_DIM = 64
DIFF_ROPE = DIFF_DIM // 4
LANES = 128
LOG2E = math.log2(math.e)

TOKEN_TILE = 512
KV_CHUNK = 512
SCORE_LANES = 2048
EXP_ROWS = 16
DENOM_ROWS = 16
MLA_QSCALE = (MLA_NOPE + MLA_ROPE) ** -0.5 * LOG2E
GQA_QSCALE = GQA_DIM ** -0.5 * LOG2E
DIFF_QSCALE = DIFF_DIM ** -0.5 * LOG2E
ADA_COL_TILE = 1536
FFN_CHUNKS = ((0, 1024), (1024, 1024), (2048, 768))
VMEM_LIMIT = 56 * 1024 * 1024


def _params(n_grid):
    return pltpu.CompilerParams(dimension_semantics=("arbitrary",) * n_grid, vmem_limit_bytes=VMEM_LIMIT)


def _const_spec(shape):
    nd = len(shape)
    return pl.BlockSpec(shape, lambda *_: (0,) * nd, pipeline_mode=pl.Buffered(1))


def _mod_spec(layer, chunk):
    return pl.BlockSpec((None, None, 1, D_MODEL), lambda b, *_: (layer, b, 0, chunk))


def _dot(a, b):
    return jnp.dot(a, b, preferred_element_type=F32)


def _rms(x):
    return x * lax.rsqrt(jnp.mean(x * x, axis=-1, keepdims=True) + EPS)


def _modnorm(x, g, scale, shift):
    return (_rms(x) * g) * (1.0 + scale) + shift


def _silu(x):
    return x * jax.nn.sigmoid(x)


def _rope(x, tab_ref, half):
    width = x.shape[-1]
    fwd = pltpu.roll(x, width - half, axis=1)
    back = pltpu.roll(x, half, axis=1)
    return x * tab_ref[0] + fwd * tab_ref[1] + back * tab_ref[2]


def _store_value_chunks(vt_ref, v, rows):
    tm = v.shape[0]
    vt = jnp.concatenate([v[:, c0:c0 + LANES].T for c0 in range(0, v.shape[1], LANES)], axis=0)
    ones = (lax.broadcasted_iota(jnp.int32, (DENOM_ROWS, tm), 0) == 0).astype(F32)
    for blk in range(v.shape[1] // rows):
        ext = jnp.concatenate([vt[blk * rows:(blk + 1) * rows], ones], axis=0).astype(BF16)
        for ci in range(tm // KV_CHUNK):
            vt_ref[0, blk, ci] = ext[:, ci * KV_CHUNK:(ci + 1) * KV_CHUNK]


def _value_chunk_out(bsz, seq, n_blk, rows):
    per_tile = TOKEN_TILE // KV_CHUNK
    spec = pl.BlockSpec((1, n_blk, per_tile, rows + DENOM_ROWS, KV_CHUNK), lambda b, t: (b, 0, t, 0, 0))
    return spec, jax.ShapeDtypeStruct((bsz, n_blk, seq // KV_CHUNK, rows + DENOM_ROWS, KV_CHUNK), BF16)


def _ada_kernel(c_ref, w_ref, b_ref, o_ref):
    sc = _silu(c_ref[...]).astype(BF16)
    o_ref[0] = _dot(sc, w_ref[0].astype(BF16)) + b_ref[0]


def _ada_call(c_all, w_ada, b_ada):
    rows = c_all.shape[0]
    n_out = w_ada.shape[-1]
    return pl.pallas_call(
        _ada_kernel,
        grid=(DEPTH, n_out // ADA_COL_TILE),
        in_specs=[
            pl.BlockSpec((rows, D_MODEL), lambda i, n: (0, 0)),
            pl.BlockSpec((1, D_MODEL, ADA_COL_TILE), lambda i, n: (i, 0, n)),
            pl.BlockSpec((1, 1, ADA_COL_TILE), lambda i, n: (i, 0, n)),
        ],
        out_specs=pl.BlockSpec((1, rows, ADA_COL_TILE), lambda i, n: (i, 0, n)),
        out_shape=jax.ShapeDtypeStruct((DEPTH, rows, n_out), F32),
        compiler_params=_params(2),
        name="ada_mod",
    )(c_all, w_ada, b_ada.reshape(DEPTH, 1, n_out))


def _ffn_kernel(*refs, has_mix, has_final):
    it = iter(refs)
    x_ref = next(it)
    if has_mix:
        ma_ref, mb_ref, wmix_ref, g1_ref = next(it), next(it), next(it), next(it)
    sh_ref, sc_ref, gt_ref, ng_ref, win_ref, wout_ref = (next(it) for _ in range(6))
    if has_final:
        fg_ref = next(it)
    o_ref = next(it)

    x = x_ref[0]
    if has_mix:
        half = wmix_ref.shape[0] // 2
        m = _dot(ma_ref[0], wmix_ref[:half, :]) + _dot(mb_ref[0], wmix_ref[half:, :])
        x = x + g1_ref[...] * m
    h = _modnorm(x, ng_ref[...], sc_ref[...], sh_ref[...]).astype(BF16)
    acc = None
    for c0, cn in FFN_CHUNKS:
        g = _dot(h, win_ref[:, c0:c0 + cn])
        u = _dot(h, win_ref[:, D_FF + c0:D_FF + c0 + cn])
        a = (_silu(g) * u).astype(BF16)
        d = _dot(a, wout_ref[c0:c0 + cn, :])
        acc = d if acc is None else acc + d
    y = x + (0.5 * gt_ref[...]) * acc
    if has_final:
        y = _rms(y) * fg_ref[...]
    o_ref[0] = y


def _ffn_call(x, mod, layer, which, norm_g_row, w_in, w_out, mix=None, final_g=None):
    bsz, seq, _ = x.shape
    tm = TOKEN_TILE
    base = 0 if which == 0 else 6
    tok = pl.BlockSpec((1, tm, D_MODEL), lambda b, t: (b, t, 0))
    args, specs = [x], [tok]
    if mix is not None:
        ma, mb, wmix = mix
        half = pl.BlockSpec((1, tm, D_MODEL // 2), lambda b, t: (b, t, 0))
        args += [ma, mb, wmix, mod]
        specs += [half, half, _const_spec(wmix.shape), _mod_spec(layer, 5)]
    args += [mod, mod, mod, norm_g_row, w_in, w_out]
    specs += [_mod_spec(layer, base), _mod_spec(layer, base + 1), _mod_spec(layer, base + 2),
              _const_spec(norm_g_row.shape), _const_spec(w_in.shape), _const_spec(w_out.shape)]
    if final_g is not None:
        args.append(final_g)
        specs.append(_const_spec(final_g.shape))
    return pl.pallas_call(
        functools.partial(_ffn_kernel, has_mix=mix is not None, has_final=final_g is not None),
        grid=(bsz, seq // tm),
        in_specs=specs,
        out_specs=tok,
        out_shape=jax.ShapeDtypeStruct(x.shape, F32),
        compiler_params=_params(2),
        name="ffn",
    )(*args)


def _pool(pz, halo, t_idx, n_tiles, seq, poolw_ref, pscale_ref):
    tm = pz.shape[0]
    prev = jnp.where(t_idx > 0, halo[:POOL_HALO], 0.0)
    nxt = jnp.where(t_idx < n_tiles - 1, halo[POOL_HALO:], 0.0)
    ext = jnp.concatenate([prev, pz, nxt], axis=0)
    n_ext = ext.shape[0]
    pos = t_idx * tm + lax.broadcasted_iota(jnp.int32, (tm, 1), 0)
    outs = []
    for g, w in enumerate(POOL_WINDOWS):
        sl = slice(g * POOL_GROUP, (g + 1) * POOL_GROUP)
        run = ext[:, sl]
        span = 1
        while span < w:
            run = run + pltpu.roll(run, n_ext - span, axis=0)
            span *= 2
        start = POOL_HALO - w // 2
        if start:
            run = pltpu.roll(run, n_ext - start, axis=0)
        win = run[:tm]
        cnt = jnp.minimum(pos + (w - w // 2), seq) - jnp.maximum(pos - w // 2, 0)
        d = (win / cnt.astype(F32) - pz[:, sl]).astype(BF16)
        outs.append(_dot(d, poolw_ref[g]))
    return jnp.concatenate(outs, axis=-1) * pscale_ref[...]


def _even_in_kernel(x_ref, xp_ref, xn_ref, sh_ref, sc_ref, ng_ref, wz_ref, gq_ref, wuq_ref, gkv_ref, wkv_ref,
                    tab_ref, poolw_ref, pscale_ref, q_ref, k_ref, v_ref, ob_ref, *, seq):
    t_idx = pl.program_id(1)
    n_tiles = pl.num_programs(1)
    g, sc, sh = ng_ref[...], sc_ref[...], sh_ref[...]
    h = _modnorm(x_ref[0], g, sc, sh).astype(BF16)
    z = _dot(h, wz_ref[...])
    o_kpe = MLA_Q_RANK + MLA_KV_RANK
    o_pz = o_kpe + LANES

    xh = jnp.concatenate([xp_ref[0], xn_ref[0]], axis=0)
    hh = _modnorm(xh, g, sc, sh).astype(BF16)
    halo = _dot(hh, wz_ref[:, o_pz:])
    ob_ref[0] = _pool(z[:, o_pz:], halo, t_idx, n_tiles, seq, poolw_ref, pscale_ref).astype(BF16)

    cq = (_rms(z[:, :MLA_Q_RANK]) * gq_ref[...]).astype(BF16)
    ckv = (_rms(z[:, MLA_Q_RANK:o_kpe]) * gkv_ref[...]).astype(BF16)
    q_all = _dot(cq, wuq_ref[...])
    kv_all = _dot(ckv, wkv_ref[...])
    kpe = _rope(z[:, o_kpe:o_pz], tab_ref, MLA_ROPE // 2)
    for hd in range(MLA_HEADS):
        sl = slice(hd * LANES, (hd + 1) * LANES)
        q_ref[0, hd] = (_rope(q_all[:, sl], tab_ref, MLA_ROPE // 2) * MLA_QSCALE).T.astype(BF16)
        k_ref[0, hd] = (kv_all[:, sl] + kpe).astype(BF16)
    _store_value_chunks(v_ref, kv_all[:, MLA_HEADS * LANES:], LANES)


def _even_in_call(x, mod, layer, norm_g_row, wz, gq, wuq, gkv, wkv, tab, poolw, pscale):
    bsz, seq, _ = x.shape
    tm = TOKEN_TILE
    hb = tm // POOL_HALO
    n_hblk = seq // POOL_HALO
    tok = pl.BlockSpec((1, tm, D_MODEL), lambda b, t: (b, t, 0))
    prev = pl.BlockSpec((1, POOL_HALO, D_MODEL), lambda b, t: (b, jnp.maximum(t * hb - 1, 0), 0))
    nxt = pl.BlockSpec((1, POOL_HALO, D_MODEL), lambda b, t: (b, jnp.minimum((t + 1) * hb, n_hblk - 1), 0))
    heads = pl.BlockSpec((1, MLA_HEADS, tm, LANES), lambda b, t: (b, 0, t, 0))
    heads_t = pl.BlockSpec((1, MLA_HEADS, LANES, tm), lambda b, t: (b, 0, 0, t))
    half = pl.BlockSpec((1, tm, D_MODEL // 2), lambda b, t: (b, t, 0))
    out_heads = jax.ShapeDtypeStruct((bsz, MLA_HEADS, seq, LANES), BF16)
    out_heads_t = jax.ShapeDtypeStruct((bsz, MLA_HEADS, LANES, seq), BF16)
    out_half = jax.ShapeDtypeStruct((bsz, seq, D_MODEL // 2), BF16)
    vt_spec, vt_shape = _value_chunk_out(bsz, seq, MLA_HEADS * MLA_V // LANES, LANES)
    return pl.pallas_call(
        functools.partial(_even_in_kernel, seq=seq),
        grid=(bsz, seq // tm),
        in_specs=[tok, prev, nxt, _mod_spec(layer, 3), _mod_spec(layer, 4), _const_spec(norm_g_row.shape),
                  _const_spec(wz.shape), _const_spec(gq.shape), _const_spec(wuq.shape), _const_spec(gkv.shape),
                  _const_spec(wkv.shape), pl.BlockSpec((3, tm, LANES), lambda b, t: (0, t, 0)),
                  _const_spec(poolw.shape), _const_spec(pscale.shape)],
        out_specs=[heads_t, heads, vt_spec, half],
        out_shape=[out_heads_t, out_heads, vt_shape, out_half],
        compiler_params=_params(2),
        name="even_in",
    )(x, x, x, mod, mod, norm_g_row, wz, gq, wuq, gkv, wkv, tab, poolw, pscale)


def _pair_norm(x, gain):
    lo_mask = lax.broadcasted_iota(jnp.int32, x.shape, 1) < GQA_DIM
    sq = x * x
    lo = jnp.sum(jnp.where(lo_mask, sq, 0.0), axis=-1, keepdims=True)
    hi = jnp.sum(jnp.where(lo_mask, 0.0, sq), axis=-1, keepdims=True)
    inv = jnp.where(lo_mask, lax.rsqrt(lo / GQA_DIM + EPS), lax.rsqrt(hi / GQA_DIM + EPS))
    return x * inv * gain


def _odd_in_kernel(x_ref, sh_ref, sc_ref, ng_ref, wz_ref, gq_ref, gk_ref, tabc_ref, tabd_ref,
                   qc_ref, kc_ref, vc_ref, qd_ref, kd_ref, vd_ref):
    h = _modnorm(x_ref[0], ng_ref[...], sc_ref[...], sh_ref[...]).astype(BF16)
    z = _dot(h, wz_ref[...])
    n_qc = GQA_HEADS * GQA_DIM // LANES
    n_kv = GQA_KV_HEADS
    n_d = DIFF_HEADS
    col = 0
    for p in range(n_qc):
        blk = _rope(_pair_norm(z[:, col:col + LANES], gq_ref[...]), tabc_ref, GQA_DIM // 4)
        qc_ref[0, p] = (blk * GQA_QSCALE).T.astype(BF16)
        col += LANES
    for j in range(n_kv):
        blk = z[:, col:col + LANES]
        kc_ref[0, j] = _rope(_pair_norm(blk, gk_ref[...]), tabc_ref, GQA_DIM // 4).astype(BF16)
        col += LANES
    _store_value_chunks(vc_ref, z[:, col:col + n_kv * GQA_DIM], GQA_DIM)
    col += n_kv * GQA_DIM
    for hd in range(n_d):
        qd_ref[0, hd] = (_rope(z[:, col:col + LANES], tabd_ref, DIFF_ROPE // 2) * DIFF_QSCALE).T.astype(BF16)
        col += LANES
    for hd in range(n_d):
        kd_ref[0, hd] = _rope(z[:, col:col + LANES], tabd_ref, DIFF_ROPE // 2).astype(BF16)
        col += LANES
    _store_value_chunks(vd_ref, z[:, col:], 2 * DIFF_DIM)


def _odd_in_call(x, mod, layer, norm_g_row, wz, gq2, gk2, tabc, tabd):
    bsz, seq, _ = x.shape
    tm = TOKEN_TILE
    tok = pl.BlockSpec((1, tm, D_MODEL), lambda b, t: (b, t, 0))
    tab = pl.BlockSpec((3, tm, LANES), lambda b, t: (0, t, 0))

    def blocks(n):
        return (pl.BlockSpec((1, n, tm, LANES), lambda b, t: (b, 0, t, 0)),
                jax.ShapeDtypeStruct((bsz, n, seq, LANES), BF16))

    def blocks_t(n):
        return (pl.BlockSpec((1, n, LANES, tm), lambda b, t: (b, 0, 0, t)),
                jax.ShapeDtypeStruct((bsz, n, LANES, seq), BF16))

    n_qc = GQA_HEADS * GQA_DIM // LANES
    outs = [blocks_t(n_qc), blocks(GQA_KV_HEADS), _value_chunk_out(bsz, seq, GQA_KV_HEADS, GQA_DIM),
            blocks_t(DIFF_HEADS), blocks(DIFF_HEADS), _value_chunk_out(bsz, seq, DIFF_HEADS, 2 * DIFF_DIM)]
    return pl.pallas_call(
        _odd_in_kernel,
        grid=(bsz, seq // tm),
        in_specs=[tok, _mod_spec(layer, 3), _mod_spec(layer, 4), _const_spec(norm_g_row.shape),
                  _const_spec(wz.shape), _const_spec(gq2.shape), _const_spec(gk2.shape), tab, tab],
        out_specs=[o[0] for o in outs],
        out_shape=[o[1] for o in outs],
        compiler_params=_params(2),
        name="odd_in",
    )(x, mod, mod, norm_g_row, wz, gq2, gk2, tabc, tabd)


def _flash_t(streams, vt_ref, n_chunks, s_bufs, p_bufs, acc_ref):
    tk, n = s_bufs[0].shape
    assert n_chunks % 2 == 0 and n_chunks >= 2
    offs, o = [], 0
    for qt, _ in streams:
        offs.append(o)
        o += qt.shape[1]
    assert o == n

    def scores(j, s_ref):
        off = pl.multiple_of(j * tk, tk)
        cms = []
        for (qt, k_ref), o in zip(streams, offs):
            s = _dot(k_ref[pl.ds(off, tk), :], qt)
            s_ref[:, o:o + qt.shape[1]] = s
            cms.append(jnp.max(s, axis=0, keepdims=True))
        return jnp.concatenate(cms, axis=1) if len(cms) > 1 else cms[0]

    def step(j, carry, cur, other, do_scores=True, do_values=True):
        m, cm, alpha_prev = carry
        m_new = jnp.maximum(m, cm)
        alpha = jnp.exp2(m - m_new)
        cm_next = scores(j + 1, s_bufs[other]) if do_scores else cm
        if do_values:
            acc_ref[...] = alpha_prev * acc_ref[...] + _dot(vt_ref[j - 1], p_bufs[other][...])
        for r in range(0, tk, EXP_ROWS):
            p_bufs[cur][r:r + EXP_ROWS, :] = jnp.exp2(s_bufs[cur][r:r + EXP_ROWS, :] - m_new).astype(BF16)
        return m_new, cm_next, alpha

    acc_ref[...] = jnp.zeros(acc_ref.shape, F32)
    cm0 = scores(0, s_bufs[0])
    carry = (jnp.full((1, n), -jnp.inf, F32), cm0, jnp.ones((1, n), F32))
    carry = step(0, carry, 0, 1, do_values=False)

    def body(jj, carry):
        carry = step(2 * jj + 1, carry, 1, 0)
        return step(2 * jj + 2, carry, 0, 1)

    carry = lax.fori_loop(0, (n_chunks - 2) // 2, body, carry)
    _, _, alpha = step(n_chunks - 1, carry, 1, 0, do_scores=False)
    return alpha * acc_ref[...] + _dot(vt_ref[n_chunks - 1], p_bufs[1][...])


def _normalised(acc, rows):
    return acc[:rows] * (1.0 / acc[rows:rows + 1])


def _row_lo(shape):
    return lax.broadcasted_iota(jnp.int32, shape, 0) < LANES // 2


def _attn_scratch(n, rows):
    return [[pltpu.VMEM((KV_CHUNK, n), F32)] * 2, [pltpu.VMEM((KV_CHUNK, n), BF16)] * 2,
            pltpu.VMEM((rows + DENOM_ROWS, n), F32)]


def _value_spec(vt):
    _, _, n_chunks, rows, tk = vt.shape
    return pl.BlockSpec((1, 1, n_chunks, rows, tk), lambda b, g, i: (b, g, 0, 0, 0))


def _mla_attn_kernel(qt_ref, k_ref, vt_ref, o_ref, s_bufs, p_bufs, acc_ref):
    tq = qt_ref.shape[3]
    acc = _flash_t([(qt_ref[0, 0], k_ref.at[0, 0]), (qt_ref[0, 1], k_ref.at[0, 1])], vt_ref.at[0, 0],
                   vt_ref.shape[2], s_bufs, p_bufs, acc_ref)
    o = _normalised(acc, LANES)
    o_ref[0] = jnp.where(_row_lo((LANES, tq)), o[:, :tq], o[:, tq:]).T.astype(BF16)


def _mla_attn_call(qt, k, vt):
    bsz, _, _, seq = qt.shape
    tq = min(SCORE_LANES // 2, seq)
    return pl.pallas_call(
        _mla_attn_kernel,
        grid=(bsz, MLA_HEADS // 2, seq // tq),
        in_specs=[pl.BlockSpec((1, 2, LANES, tq), lambda b, p, i: (b, p, 0, i)),
                  pl.BlockSpec((1, 2, seq, LANES), lambda b, p, i: (b, p, 0, 0)),
                  _value_spec(vt)],
        out_specs=pl.BlockSpec((1, tq, LANES), lambda b, p, i: (b, i, p)),
        out_shape=jax.ShapeDtypeStruct((bsz, seq, D_MODEL // 2), BF16),
        scratch_shapes=_attn_scratch(2 * tq, LANES),
        compiler_params=_params(3),
        name="mla_attn",
    )(qt, k, vt)


def _split_halves_t(qt_ref):
    parts = []
    for p in range(qt_ref.shape[1]):
        qp = qt_ref[0, p]
        lo = _row_lo(qp.shape)
        parts += [jnp.where(lo, qp, 0), jnp.where(lo, 0, qp)]
    return jnp.concatenate(parts, axis=1).astype(BF16)


def _gqa_attn_kernel(qt_ref, k_ref, vt_ref, o_ref, s_bufs, p_bufs, acc_ref):
    n_pairs, tq = qt_ref.shape[1], qt_ref.shape[3]
    acc = _flash_t([(_split_halves_t(qt_ref), k_ref.at[0, 0])], vt_ref.at[0, 0], vt_ref.shape[2],
                   s_bufs, p_bufs, acc_ref)
    o = _normalised(acc, GQA_DIM)
    for p in range(n_pairs):
        pair = jnp.concatenate([o[:, (2 * p) * tq:(2 * p + 1) * tq], o[:, (2 * p + 1) * tq:(2 * p + 2) * tq]], axis=0)
        o_ref[0, :, p * LANES:(p + 1) * LANES] = pair.T.astype(BF16)


def _gqa_attn_call(qt, k, vt):
    bsz, n_qblk, _, seq = qt.shape
    per_kv = n_qblk // GQA_KV_HEADS
    tq = min(SCORE_LANES // (2 * per_kv), seq)
    return pl.pallas_call(
        _gqa_attn_kernel,
        grid=(bsz, GQA_KV_HEADS, seq // tq),
        in_specs=[pl.BlockSpec((1, per_kv, LANES, tq), lambda b, j, i: (b, j, 0, i)),
                  pl.BlockSpec((1, 1, seq, LANES), lambda b, j, i: (b, j, 0, 0)),
                  _value_spec(vt)],
        out_specs=pl.BlockSpec((1, tq, per_kv * LANES), lambda b, j, i: (b, i, j)),
        out_shape=jax.ShapeDtypeStruct((bsz, seq, D_MODEL // 2), BF16),
        scratch_shapes=_attn_scratch(2 * per_kv * tq, GQA_DIM),
        compiler_params=_params(3),
        name="gqa_attn",
    )(qt, k, vt)


def _diff_attn_kernel(qt_ref, k_ref, vt_ref, lq1_ref, lk1_ref, lq2_ref, lk2_ref, sg_ref, o_ref, s_bufs, p_bufs,
                      acc_ref, *, lam_init):
    tq = qt_ref.shape[3]
    acc = _flash_t([(_split_halves_t(qt_ref), k_ref.at[0, 0])], vt_ref.at[0, 0], vt_ref.shape[2],
                   s_bufs, p_bufs, acc_ref)
    o = _normalised(acc, 2 * DIFF_DIM)
    lam = (jnp.exp(jnp.sum(lq1_ref[...] * lk1_ref[...], axis=-1, keepdims=True))
           - jnp.exp(jnp.sum(lq2_ref[...] * lk2_ref[...], axis=-1, keepdims=True)) + lam_init)
    od = o[:, :tq] - lam * o[:, tq:]
    od = od * lax.rsqrt(jnp.mean(od * od, axis=0, keepdims=True) + EPS)
    o_ref[0] = ((od * sg_ref[...]) * (1.0 - lam_init)).T.astype(BF16)


def _diff_attn_call(qt, k, vt, lq1, lk1, lq2, lk2, subln_g_col, lam_init):
    bsz, n_heads, _, seq = qt.shape
    tq = min(SCORE_LANES // 2, seq)
    return pl.pallas_call(
        functools.partial(_diff_attn_kernel, lam_init=lam_init),
        grid=(bsz, n_heads, seq // tq),
        in_specs=[pl.BlockSpec((1, 1, LANES, tq), lambda b, h, i: (b, h, 0, i)),
                  pl.BlockSpec((1, 1, seq, LANES), lambda b, h, i: (b, h, 0, 0)),
                  _value_spec(vt),
                  _const_spec(lq1.shape), _const_spec(lk1.shape), _const_spec(lq2.shape), _const_spec(lk2.shape),
                  _const_spec(subln_g_col.shape)],
        out_specs=pl.BlockSpec((1, tq, LANES), lambda b, h, i: (b, i, h)),
        out_shape=jax.ShapeDtypeStruct((bsz, seq, D_MODEL // 2), BF16),
        scratch_shapes=_attn_scratch(2 * tq, 2 * DIFF_DIM),
        compiler_params=_params(3),
        name="diff_attn",
    )(qt, k, vt, lq1, lk1, lq2, lk2, subln_g_col)


def _pad_cols(w, n):
    return jnp.pad(w, ((0, 0), (0, n)))


def _prep_even(ev_w_in, mla_w_uq, mla_w_ukv):
    o_kpe = MLA_Q_RANK + MLA_KV_RANK
    wz = jnp.concatenate([
        ev_w_in[:, :o_kpe],
        jnp.zeros((D_MODEL, MLA_NOPE), F32), ev_w_in[:, o_kpe:o_kpe + MLA_ROPE],
        jnp.zeros((D_MODEL, LANES - MLA_NOPE - MLA_ROPE), F32),
        ev_w_in[:, o_kpe + MLA_ROPE:]], axis=1)
    wuq = mla_w_uq.reshape(MLA_Q_RANK, MLA_HEADS, MLA_NOPE + MLA_ROPE)
    wuq = jnp.pad(wuq, ((0, 0), (0, 0), (0, LANES - MLA_NOPE - MLA_ROPE))).reshape(MLA_Q_RANK, MLA_HEADS * LANES)
    wukv = mla_w_ukv.reshape(MLA_KV_RANK, MLA_HEADS, MLA_NOPE + MLA_V)
    wuk = jnp.pad(wukv[:, :, :MLA_NOPE], ((0, 0), (0, 0), (0, LANES - MLA_NOPE)))
    wkv = jnp.concatenate([wuk.reshape(MLA_KV_RANK, MLA_HEADS * LANES),
                           wukv[:, :, MLA_NOPE:].reshape(MLA_KV_RANK, MLA_HEADS * MLA_V)], axis=1)
    return wz.astype(BF16), wuq.astype(BF16), wkv.astype(BF16)


def _prep_odd(od_w_in):
    n_qc = GQA_HEADS * GQA_DIM
    n_kv = GQA_KV_HEADS * GQA_DIM

    def dup(w):
        w = w.reshape(D_MODEL, GQA_KV_HEADS, 1, GQA_DIM)
        return jnp.broadcast_to(w, (D_MODEL, GQA_KV_HEADS, 2, GQA_DIM)).reshape(D_MODEL, 2 * n_kv)

    wz = jnp.concatenate([od_w_in[:, :n_qc], dup(od_w_in[:, n_qc:n_qc + n_kv]), od_w_in[:, n_qc + n_kv:]], axis=1)
    return wz.astype(BF16)


def _rope_tables(seq):
    pos = jnp.arange(seq)

    def angles(p, n_freq, theta):
        inv = theta ** (-jnp.arange(0, n_freq, dtype=F32) * 2.0 / (2 * n_freq))
        ang = p.astype(F32)[:, None] * inv[None, :]
        return jnp.cos(ang), jnp.sin(ang)

    one = lambda n: jnp.ones((seq, n), F32)
    zero = lambda n: jnp.zeros((seq, n), F32)

    def stack(c, s1, s2):
        reps = LANES // c.shape[1]
        return jnp.stack([jnp.tile(t, (1, reps)) for t in (c, s1, s2)])

    cos, sin = angles(pos, MLA_ROPE // 2, ROPE_THETA)
    pad = LANES - MLA_NOPE - MLA_ROPE
    mla = stack(jnp.concatenate([one(MLA_NOPE), cos, cos, one(pad)], 1),
                jnp.concatenate([zero(MLA_NOPE), -sin, zero(MLA_ROPE // 2), zero(pad)], 1),
                jnp.concatenate([zero(MLA_NOPE), zero(MLA_ROPE // 2), sin, zero(pad)], 1))
    q = GQA_DIM // 4
    cr, sr = angles(pos // GRID_W, q, AXIAL_THETA)
    cc, sc = angles(pos % GRID_W, q, AXIAL_THETA)
    axial = stack(jnp.concatenate([cr, cr, cc, cc], 1),
                  jnp.concatenate([-sr, zero(q), -sc, zero(q)], 1),
                  jnp.concatenate([zero(q), sr, zero(q), sc], 1))
    hr = DIFF_ROPE // 2
    cd, sd = angles(pos, hr, ROPE_THETA)
    rest = DIFF_DIM - DIFF_ROPE
    diff = stack(jnp.concatenate([cd, cd, one(rest)], 1),
                 jnp.concatenate([-sd, zero(hr), zero(rest)], 1),
                 jnp.concatenate([zero(hr), sd, zero(rest)], 1))
    return mla, axial, diff


def kernel(x_prompt, x_sample, c_prompt, c_sample, w_ada, b_ada, norm_g, ffn_w_in, ffn_w_out, ev_w_in, mla_gq, mla_w_uq, mla_gkv, mla_w_ukv, pool_w, pool_scale, ev_w_out, od_w_in, gqa_gq, gqa_gk, diff_lq1, diff_lk1, diff_lq2, diff_lk2, diff_subln_g, od_w_out, final_g):
    groups = [x_prompt, x_sample]
    n_rows = [x.shape[0] for x in groups]
    c_all = jnp.concatenate([c_prompt, c_sample], axis=0)
    c_all = jnp.pad(c_all, ((0, -c_all.shape[0] % 16), (0, 0)))
    mod = _ada_call(c_all, w_ada, b_ada)
    mods, r0 = [], 0
    for n in n_rows:
        mods.append(mod[:, r0:r0 + n].reshape(DEPTH, n, 1, 9 * D_MODEL))
        r0 += n

    ffn_in = ffn_w_in.astype(BF16)
    ffn_out = ffn_w_out.astype(BF16)
    row = lambda v: v.reshape(1, -1)
    tables = [_rope_tables(x.shape[1]) for x in groups]

    xs = list(groups)
    for i in range(DEPTH):
        j = i // 2
        if i % 2 == 0:
            wz, wuq, wkv = _prep_even(ev_w_in[j], mla_w_uq[j], mla_w_ukv[j])
            w_mix = ev_w_out[j].astype(BF16)
            poolw = pool_w[j].astype(BF16)
        else:
            wz = _prep_odd(od_w_in[j])
            w_mix = od_w_out[j].astype(BF16)
            gq2 = row(jnp.tile(gqa_gq[j], 2))
            gk2 = row(jnp.tile(gqa_gk[j], 2))
            lam_init = 0.8 - 0.6 * math.exp(-0.3 * i)
        for gi in range(len(groups)):
            x, md = xs[gi], mods[gi]
            tab_mla, tab_axial, tab_diff = tables[gi]
            x = _ffn_call(x, md, i, 0, row(norm_g[i, 0]), ffn_in[i, 0], ffn_out[i, 0])
            if i % 2 == 0:
                q, k, v, ob = _even_in_call(x, md, i, row(norm_g[i, 1]), wz, row(mla_gq[j]), wuq, row(mla_gkv[j]),
                                            wkv, tab_mla, poolw, row(pool_scale[j]))
                mix = (_mla_attn_call(q, k, v), ob, w_mix)
            else:
                qc, kc, vc, qd, kd, vd = _odd_in_call(x, md, i, row(norm_g[i, 1]), wz, gq2, gk2, tab_axial, tab_diff)
                oc = _gqa_attn_call(qc, kc, vc)
                od = _diff_attn_call(qd, kd, vd, row(diff_lq1[j]), row(diff_lk1[j]), row(diff_lq2[j]),
                                     row(diff_lk2[j]), diff_subln_g[j].reshape(-1, 1), lam_init)
                mix = (oc, od, w_mix)
            xs[gi] = _ffn_call(x, md, i, 1, row(norm_g[i, 2]), ffn_in[i, 1], ffn_out[i, 1], mix=mix,
                               final_g=row(final_g) if i == DEPTH - 1 else None)
    return tuple(xs)
```

```python
import functools
import math

import jax
import jax.numpy as jnp
from jax import lax
from jax.experimental import pallas as pl
from jax.experimental.pallas import tpu as pltpu

F32 = jnp.float32
BF16 = jnp.bfloat16

D_MODEL = 1024
DEPTH = 4
GRID_W = 64
EPS = 1e-6
D_FF = 2816
ROPE_THETA = 500000.0
AXIAL_THETA = 10000.0
MLA_HEADS = 8
MLA_NOPE = 64
MLA_ROPE = 32
MLA_V = 64
MLA_Q_RANK = 384
MLA_KV_RANK = 256
POOL_WINDOWS = (2, 4, 8, 16)
POOL_GROUP = 128
POOL_WIDTH = POOL_GROUP * len(POOL_WINDOWS)
POOL_HALO = 8
GQA_HEADS = 8
GQA_KV_HEADS = 2
GQA_DIM = 64
DIFF_HEADS = 4
DIFF_DIM = 64
DIFF_ROPE = DIFF_DIM // 4
LANES = 128
LOG2E = math.log2(math.e)

TOKEN_TILE = 512
KV_CHUNK = 512
SCORE_LANES = 2048
ATTN_KEYS_PER_STEP = 8192
EXP_ROWS = 16
DENOM_ROWS = 16
MLA_QSCALE = (MLA_NOPE + MLA_ROPE) ** -0.5 * LOG2E
GQA_QSCALE = GQA_DIM ** -0.5 * LOG2E
DIFF_QSCALE = DIFF_DIM ** -0.5 * LOG2E
ADA_COL_TILE = 1536
FFN_CHUNKS = ((0, 1024), (1024, 1024), (2048, 768))
VMEM_LIMIT = 56 * 1024 * 1024


def _params(n_grid):
    return pltpu.CompilerParams(dimension_semantics=("arbitrary",) * n_grid, vmem_limit_bytes=VMEM_LIMIT)


def _const_spec(shape):
    nd = len(shape)
    return pl.BlockSpec(shape, lambda *_: (0,) * nd, pipeline_mode=pl.Buffered(1))


def _mod_spec(layer, chunk):
    return pl.BlockSpec((None, None, 1, D_MODEL), lambda b, *_: (layer, b, 0, chunk))


def _dot(a, b):
    return jnp.dot(a, b, preferred_element_type=F32)


def _rms(x):
    return x * lax.rsqrt(jnp.mean(x * x, axis=-1, keepdims=True) + EPS)


def _modnorm(x, g, scale, shift):
    return (_rms(x) * g) * (1.0 + scale) + shift


def _silu(x):
    return x * jax.nn.sigmoid(x)


def _rope(x, tab_ref, half):
    width = x.shape[-1]
    fwd = pltpu.roll(x, width - half, axis=1)
    back = pltpu.roll(x, half, axis=1)
    return x * tab_ref[0] + fwd * tab_ref[1] + back * tab_ref[2]


def _store_value_chunks(vt_ref, v, rows):
    tm = v.shape[0]
    vt = jnp.concatenate([v[:, c0:c0 + LANES].T for c0 in range(0, v.shape[1], LANES)], axis=0)
    ones = (lax.broadcasted_iota(jnp.int32, (DENOM_ROWS, tm), 0) == 0).astype(F32)
    for blk in range(v.shape[1] // rows):
        ext = jnp.concatenate([vt[blk * rows:(blk + 1) * rows], ones], axis=0).astype(BF16)
        vt_ref[0, blk, 0] = ext


def _query_tile(seq, n_streams):
    tq = min(SCORE_LANES // n_streams, seq)
    assert tq % TOKEN_TILE == 0 and seq % tq == 0
    return tq


def _query_block_out(bsz, seq, n_blk, tq):
    per_blk = tq // TOKEN_TILE
    spec = pl.BlockSpec((1, n_blk, 1, LANES, TOKEN_TILE), lambda b, t: (b, 0, t // per_blk, 0, t % per_blk))
    return spec, jax.ShapeDtypeStruct((bsz, n_blk, seq // tq, LANES, tq), BF16)


def _value_chunk_out(bsz, seq, n_blk, rows):
    per_chunk = KV_CHUNK // TOKEN_TILE
    spec = pl.BlockSpec((1, n_blk, 1, rows + DENOM_ROWS, TOKEN_TILE),
                        lambda b, t: (b, 0, t // per_chunk, 0, t % per_chunk))
    return spec, jax.ShapeDtypeStruct((bsz, n_blk, seq // KV_CHUNK, rows + DENOM_ROWS, KV_CHUNK), BF16)


def _ada_kernel(c_ref, w_ref, b_ref, o_ref):
    sc = _silu(c_ref[...]).astype(BF16)
    o_ref[0] = _dot(sc, w_ref[0].astype(BF16)) + b_ref[0]


def _ada_call(c_all, w_ada, b_ada):
    rows = c_all.shape[0]
    n_out = w_ada.shape[-1]
    return pl.pallas_call(
        _ada_kernel,
        grid=(DEPTH, n_out // ADA_COL_TILE),
        in_specs=[
            pl.BlockSpec((rows, D_MODEL), lambda i, n: (0, 0)),
            pl.BlockSpec((1, D_MODEL, ADA_COL_TILE), lambda i, n: (i, 0, n)),
            pl.BlockSpec((1, 1, ADA_COL_TILE), lambda i, n: (i, 0, n)),
        ],
        out_specs=pl.BlockSpec((1, rows, ADA_COL_TILE), lambda i, n: (i, 0, n)),
        out_shape=jax.ShapeDtypeStruct((DEPTH, rows, n_out), F32),
        compiler_params=_params(2),
        name="ada_mod",
    )(c_all, w_ada, b_ada.reshape(DEPTH, 1, n_out))


def _ffn_kernel(*refs, has_mix, has_final):
    it = iter(refs)
    x_ref = next(it)
    if has_mix:
        ma_ref, mb_ref, wmix_ref, g1_ref = next(it), next(it), next(it), next(it)
    sh_ref, sc_ref, gt_ref, ng_ref, win_ref, wout_ref = (next(it) for _ in range(6))
    if has_final:
        fg_ref = next(it)
    o_ref = next(it)

    x = x_ref[0]
    if has_mix:
        half = wmix_ref.shape[0] // 2
        m = _dot(ma_ref[0], wmix_ref[:half, :]) + _dot(mb_ref[0], wmix_ref[half:, :])
        x = x + g1_ref[...] * m
    h = _modnorm(x, ng_ref[...], sc_ref[...], sh_ref[...]).astype(BF16)
    acc = None
    for c0, cn in FFN_CHUNKS:
        g = _dot(h, win_ref[:, c0:c0 + cn])
        u = _dot(h, win_ref[:, D_FF + c0:D_FF + c0 + cn])
        a = (_silu(g) * u).astype(BF16)
        d = _dot(a, wout_ref[c0:c0 + cn, :])
        acc = d if acc is None else acc + d
    y = x + (0.5 * gt_ref[...]) * acc
    if has_final:
        y = _rms(y) * fg_ref[...]
    o_ref[0] = y


def _ffn_call(x, mod, layer, which, norm_g_row, w_in, w_out, mix=None, final_g=None):
    bsz, seq, _ = x.shape
    tm = TOKEN_TILE
    base = 0 if which == 0 else 6
    tok = pl.BlockSpec((1, tm, D_MODEL), lambda b, t: (b, t, 0))
    args, specs = [x], [tok]
    if mix is not None:
        ma, mb, wmix = mix
        half = pl.BlockSpec((1, tm, D_MODEL // 2), lambda b, t: (b, t, 0))
        args += [ma, mb, wmix, mod]
        specs += [half, half, _const_spec(wmix.shape), _mod_spec(layer, 5)]
    args += [mod, mod, mod, norm_g_row, w_in, w_out]
    specs += [_mod_spec(layer, base), _mod_spec(layer, base + 1), _mod_spec(layer, base + 2),
              _const_spec(norm_g_row.shape), _const_spec(w_in.shape), _const_spec(w_out.shape)]
    if final_g is not None:
        args.append(final_g)
        specs.append(_const_spec(final_g.shape))
    return pl.pallas_call(
        functools.partial(_ffn_kernel, has_mix=mix is not None, has_final=final_g is not None),
        grid=(bsz, seq // tm),
        in_specs=specs,
        out_specs=tok,
        out_shape=jax.ShapeDtypeStruct(x.shape, F32),
        compiler_params=_params(2),
        name="ffn",
    )(*args)


def _pool(pz, halo, t_idx, n_tiles, seq, poolw_ref, pscale_ref):
    tm = pz.shape[0]
    prev = jnp.where(t_idx > 0, halo[:POOL_HALO], 0.0)
    nxt = jnp.where(t_idx < n_tiles - 1, halo[POOL_HALO:], 0.0)
    ext = jnp.concatenate([prev, pz, nxt], axis=0)
    n_ext = ext.shape[0]
    pos = t_idx * tm + lax.broadcasted_iota(jnp.int32, (tm, 1), 0)
    outs = []
    for g, w in enumerate(POOL_WINDOWS):
        sl = slice(g * POOL_GROUP, (g + 1) * POOL_GROUP)
        run = ext[:, sl]
        span = 1
        while span < w:
            run = run + pltpu.roll(run, n_ext - span, axis=0)
            span *= 2
        start = POOL_HALO - w // 2
        if start:
            run = pltpu.roll(run, n_ext - start, axis=0)
        win = run[:tm]
        cnt = jnp.minimum(pos + (w - w // 2), seq) - jnp.maximum(pos - w // 2, 0)
        d = (win / cnt.astype(F32) - pz[:, sl]).astype(BF16)
        outs.append(_dot(d, poolw_ref[g]))
    return jnp.concatenate(outs, axis=-1) * pscale_ref[...]


def _even_in_kernel(x_ref, xp_ref, xn_ref, sh_ref, sc_ref, ng_ref, wz_ref, gq_ref, wuq_ref, gkv_ref, wkv_ref,
                    tab_ref, poolw_ref, pscale_ref, q_ref, k_ref, v_ref, ob_ref, *, seq):
    t_idx = pl.program_id(1)
    n_tiles = pl.num_programs(1)
    g, sc, sh = ng_ref[...], sc_ref[...], sh_ref[...]
    h = _modnorm(x_ref[0], g, sc, sh).astype(BF16)
    z = _dot(h, wz_ref[...])
    o_kpe = MLA_Q_RANK + MLA_KV_RANK
    o_pz = o_kpe + LANES

    xh = jnp.concatenate([xp_ref[0], xn_ref[0]], axis=0)
    hh = _modnorm(xh, g, sc, sh).astype(BF16)
    halo = _dot(hh, wz_ref[:, o_pz:])
    ob_ref[0] = _pool(z[:, o_pz:], halo, t_idx, n_tiles, seq, poolw_ref, pscale_ref).astype(BF16)

    cq = (_rms(z[:, :MLA_Q_RANK]) * gq_ref[...]).astype(BF16)
    ckv = (_rms(z[:, MLA_Q_RANK:o_kpe]) * gkv_ref[...]).astype(BF16)
    q_all = _dot(cq, wuq_ref[...])
    kv_all = _dot(ckv, wkv_ref[...])
    kpe = _rope(z[:, o_kpe:o_pz], tab_ref, MLA_ROPE // 2)
    for hd in range(MLA_HEADS):
        sl = slice(hd * LANES, (hd + 1) * LANES)
        q_ref[0, hd, 0] = (_rope(q_all[:, sl], tab_ref, MLA_ROPE // 2) * MLA_QSCALE).T.astype(BF16)
        k_ref[0, hd] = (kv_all[:, sl] + kpe).astype(BF16)
    _store_value_chunks(v_ref, kv_all[:, MLA_HEADS * LANES:], MLA_V)


def _even_in_call(x, mod, layer, norm_g_row, wz, gq, wuq, gkv, wkv, tab, poolw, pscale):
    bsz, seq, _ = x.shape
    tm = TOKEN_TILE
    hb = tm // POOL_HALO
    n_hblk = seq // POOL_HALO
    tok = pl.BlockSpec((1, tm, D_MODEL), lambda b, t: (b, t, 0))
    prev = pl.BlockSpec((1, POOL_HALO, D_MODEL), lambda b, t: (b, jnp.maximum(t * hb - 1, 0), 0))
    nxt = pl.BlockSpec((1, POOL_HALO, D_MODEL), lambda b, t: (b, jnp.minimum((t + 1) * hb, n_hblk - 1), 0))
    heads = pl.BlockSpec((1, MLA_HEADS, tm, LANES), lambda b, t: (b, 0, t, 0))
    heads_t, out_heads_t = _query_block_out(bsz, seq, MLA_HEADS, _query_tile(seq, 2))
    half = pl.BlockSpec((1, tm, D_MODEL // 2), lambda b, t: (b, t, 0))
    out_heads = jax.ShapeDtypeStruct((bsz, MLA_HEADS, seq, LANES), BF16)
    out_half = jax.ShapeDtypeStruct((bsz, seq, D_MODEL // 2), BF16)
    vt_spec, vt_shape = _value_chunk_out(bsz, seq, MLA_HEADS, MLA_V)
    return pl.pallas_call(
        functools.partial(_even_in_kernel, seq=seq),
        grid=(bsz, seq // tm),
        in_specs=[tok, prev, nxt, _mod_spec(layer, 3), _mod_spec(layer, 4), _const_spec(norm_g_row.shape),
                  _const_spec(wz.shape), _const_spec(gq.shape), _const_spec(wuq.shape), _const_spec(gkv.shape),
                  _const_spec(wkv.shape), pl.BlockSpec((3, tm, LANES), lambda b, t: (0, t, 0)),
                  _const_spec(poolw.shape), _const_spec(pscale.shape)],
        out_specs=[heads_t, heads, vt_spec, half],
        out_shape=[out_heads_t, out_heads, vt_shape, out_half],
        compiler_params=_params(2),
        name="even_in",
    )(x, x, x, mod, mod, norm_g_row, wz, gq, wuq, gkv, wkv, tab, poolw, pscale)


def _pair_norm(x, gain):
    lo_mask = lax.broadcasted_iota(jnp.int32, x.shape, 1) < GQA_DIM
    sq = x * x
    lo = jnp.sum(jnp.where(lo_mask, sq, 0.0), axis=-1, keepdims=True)
    hi = jnp.sum(jnp.where(lo_mask, 0.0, sq), axis=-1, keepdims=True)
    inv = jnp.where(lo_mask, lax.rsqrt(lo / GQA_DIM + EPS), lax.rsqrt(hi / GQA_DIM + EPS))
    return x * inv * gain


def _odd_in_kernel(x_ref, sh_ref, sc_ref, ng_ref, wz_ref, gq_ref, gk_ref, tabc_ref, tabd_ref,
                   qc_ref, kc_ref, vc_ref, qd_ref, kd_ref, vd_ref):
    h = _modnorm(x_ref[0], ng_ref[...], sc_ref[...], sh_ref[...]).astype(BF16)
    z = _dot(h, wz_ref[...])
    n_qc = GQA_HEADS * GQA_DIM // LANES
    n_kv = GQA_KV_HEADS
    n_d = DIFF_HEADS
    col = 0
    for p in range(n_qc):
        blk = _rope(_pair_norm(z[:, col:col + LANES], gq_ref[...]), tabc_ref, GQA_DIM // 4)
        qc_ref[0, p, 0] = (blk * GQA_QSCALE).T.astype(BF16)
        col += LANES
    for j in range(n_kv):
        blk = z[:, col:col + LANES]
        kc_ref[0, j] = _rope(_pair_norm(blk, gk_ref[...]), tabc_ref, GQA_DIM // 4).astype(BF16)
        col += LANES
    _store_value_chunks(vc_ref, z[:, col:col + n_kv * GQA_DIM], GQA_DIM)
    col += n_kv * GQA_DIM
    for hd in range(n_d):
        qd_ref[0, hd, 0] = (_rope(z[:, col:col + LANES], tabd_ref, DIFF_ROPE // 2) * DIFF_QSCALE).T.astype(BF16)
        col += LANES
    for hd in range(n_d):
        kd_ref[0, hd] = _rope(z[:, col:col + LANES], tabd_ref, DIFF_ROPE // 2).astype(BF16)
        col += LANES
    _store_value_chunks(vd_ref, z[:, col:], 2 * DIFF_DIM)


def _odd_in_call(x, mod, layer, norm_g_row, wz, gq2, gk2, tabc, tabd):
    bsz, seq, _ = x.shape
    tm = TOKEN_TILE
    tok = pl.BlockSpec((1, tm, D_MODEL), lambda b, t: (b, t, 0))
    tab = pl.BlockSpec((3, tm, LANES), lambda b, t: (0, t, 0))

    def blocks(n):
        return (pl.BlockSpec((1, n, tm, LANES), lambda b, t: (b, 0, t, 0)),
                jax.ShapeDtypeStruct((bsz, n, seq, LANES), BF16))

    n_qc = GQA_HEADS * GQA_DIM // LANES
    outs = [_query_block_out(bsz, seq, n_qc, _query_tile(seq, 2 * n_qc // GQA_KV_HEADS)), blocks(GQA_KV_HEADS),
            _value_chunk_out(bsz, seq, GQA_KV_HEADS, GQA_DIM),
            _query_block_out(bsz, seq, DIFF_HEADS, _query_tile(seq, 2)), blocks(DIFF_HEADS),
            _value_chunk_out(bsz, seq, DIFF_HEADS, 2 * DIFF_DIM)]
    return pl.pallas_call(
        _odd_in_kernel,
        grid=(bsz, seq // tm),
        in_specs=[tok, _mod_spec(layer, 3), _mod_spec(layer, 4), _const_spec(norm_g_row.shape),
                  _const_spec(wz.shape), _const_spec(gq2.shape), _const_spec(gk2.shape), tab, tab],
        out_specs=[o[0] for o in outs],
        out_shape=[o[1] for o in outs],
        compiler_params=_params(2),
        name="odd_in",
    )(x, mod, mod, norm_g_row, wz, gq2, gk2, tabc, tabd)


def _attention_pipeline(n_units, n_chunks, lanes, load_q, stream_refs, finalize, q_scr, s_bufs, p_bufs, acc_ref):
    tk, n = s_bufs[0].shape
    assert n_chunks % 2 == 0 and n_chunks >= 2

    def cat(parts):
        return jnp.concatenate(parts, axis=1) if len(parts) > 1 else parts[0]

    def scores(u, j, s_ref):
        off = pl.multiple_of(j * tk, tk)
        cms = []
        for (k_ref, _), ln in zip(stream_refs(u), lanes):
            s = _dot(k_ref[pl.ds(off, tk), :], q_scr[:, ln])
            s_ref[:, ln] = s
            cms.append(jnp.max(s, axis=0, keepdims=True))
        return cat(cms)

    def values(u, j, p_ref, alpha):
        return cat([alpha[:, ln] * acc_ref[:, ln] + _dot(vt_ref[j], p_ref[:, ln])
                    for (_, vt_ref), ln in zip(stream_refs(u), lanes)])

    def exponentials(cur, m_new):
        for r in range(0, tk, EXP_ROWS):
            p_bufs[cur][r:r + EXP_ROWS, :] = jnp.exp2(s_bufs[cur][r:r + EXP_ROWS, :] - m_new).astype(BF16)

    def step(u, j, carry, cur, other):
        m, cm, alpha_prev = carry
        m_new = jnp.maximum(m, cm)
        alpha = jnp.exp2(m - m_new)
        cm_next = scores(u, j + 1, s_bufs[other])
        acc_ref[...] = values(u, j - 1, p_bufs[other], alpha_prev)
        exponentials(cur, m_new)
        return m_new, cm_next, alpha

    def first_step(u, cm):
        cm1 = scores(u, 1, s_bufs[1])
        acc_ref[...] = jnp.zeros(acc_ref.shape, F32)
        exponentials(0, cm)
        return cm, cm1, jnp.ones((1, n), F32)

    def unit(u, carry):
        def body(jj, carry):
            carry = step(u, 2 * jj + 1, carry, 1, 0)
            return step(u, 2 * jj + 2, carry, 0, 1)

        m, cm_last, alpha_prev = lax.fori_loop(0, (n_chunks - 2) // 2, body, carry)
        m_new = jnp.maximum(m, cm_last)
        alpha = jnp.exp2(m - m_new)
        nxt = jnp.minimum(u + 1, n_units - 1)
        q_scr[...] = load_q(nxt)
        cm_next = scores(nxt, 0, s_bufs[0])
        acc_ref[...] = values(u, n_chunks - 2, p_bufs[0], alpha_prev)
        exponentials(1, m_new)
        finalize(u, values(u, n_chunks - 1, p_bufs[1], alpha))
        return first_step(nxt, cm_next)

    q_scr[...] = load_q(0)
    lax.fori_loop(0, n_units, unit, first_step(0, scores(0, 0, s_bufs[0])))


def _normalised(acc, rows):
    return acc[:rows] * (1.0 / acc[rows:rows + 1])


def _row_lo(shape):
    return lax.broadcasted_iota(jnp.int32, shape, 0) < LANES // 2


def _split_halves_t(qp):
    lo = _row_lo(qp.shape)
    return jnp.concatenate([jnp.where(lo, qp, 0), jnp.where(lo, 0, qp)], axis=1).astype(BF16)


def _attn_call(name, kernel_fn, qt, k, vt, extra, heads_per_step, q_blocks_per_step, out_lanes):
    bsz, n_qblk, n_q, _, tq = qt.shape
    seq = k.shape[2]
    n_groups = n_qblk // q_blocks_per_step
    k_per, v_per = k.shape[1] // n_groups, vt.shape[1] // n_groups
    nb = max(1, min(bsz, ATTN_KEYS_PER_STEP // seq))
    assert bsz % nb == 0
    n_lanes = heads_per_step * tq
    rows = vt.shape[3]
    return pl.pallas_call(
        functools.partial(kernel_fn, n_units=nb * n_q, n_q=n_q),
        grid=(bsz // nb, n_groups),
        in_specs=[pl.BlockSpec((nb, q_blocks_per_step, n_q, LANES, tq), lambda b, g: (b, g, 0, 0, 0)),
                  pl.BlockSpec((nb, k_per, seq, LANES), lambda b, g: (b, g, 0, 0)),
                  pl.BlockSpec((nb, v_per) + vt.shape[2:], lambda b, g: (b, g, 0, 0, 0))]
        + [_const_spec(e.shape) for e in extra],
        out_specs=pl.BlockSpec((nb, seq, out_lanes), lambda b, g: (b, 0, g)),
        out_shape=jax.ShapeDtypeStruct((bsz, seq, D_MODEL // 2), BF16),
        scratch_shapes=[pltpu.VMEM((LANES, n_lanes), BF16), [pltpu.VMEM((KV_CHUNK, n_lanes), F32)] * 2,
                        [pltpu.VMEM((KV_CHUNK, n_lanes), BF16)] * 2, pltpu.VMEM((rows, n_lanes), F32)],
        compiler_params=_params(2),
        name=name,
    )(qt, k, vt, *extra)


def _mla_attn_kernel(qt_ref, k_ref, vt_ref, o_ref, q_scr, s_bufs, p_bufs, acc_ref, *, n_units, n_q):
    tq = qt_ref.shape[4]

    def finalize(u, acc):
        o = _normalised(acc, MLA_V)
        o_ref[u // n_q, pl.ds(pl.multiple_of((u % n_q) * tq, tq), tq), :] = (
            jnp.concatenate([o[:, :tq], o[:, tq:]], axis=0).T.astype(BF16))

    _attention_pipeline(
        n_units, vt_ref.shape[2], [slice(0, tq), slice(tq, 2 * tq)],
        lambda u: jnp.concatenate([qt_ref[u // n_q, h, u % n_q] for h in range(2)], axis=1),
        lambda u: [(k_ref.at[u // n_q, h], vt_ref.at[u // n_q, h]) for h in range(2)],
        finalize, q_scr, s_bufs, p_bufs, acc_ref)


def _mla_attn_call(qt, k, vt):
    return _attn_call("mla_attn", _mla_attn_kernel, qt, k, vt, [], 2, 2, LANES)


def _gqa_attn_kernel(qt_ref, k_ref, vt_ref, o_ref, q_scr, s_bufs, p_bufs, acc_ref, *, n_units, n_q):
    n_pairs, tq = qt_ref.shape[1], qt_ref.shape[4]

    def finalize(u, acc):
        o = _normalised(acc, GQA_DIM)
        rows = pl.ds(pl.multiple_of((u % n_q) * tq, tq), tq)
        for p in range(n_pairs):
            pair = jnp.concatenate([o[:, (2 * p) * tq:(2 * p + 1) * tq], o[:, (2 * p + 1) * tq:(2 * p + 2) * tq]],
                                   axis=0)
            o_ref[u // n_q, rows, p * LANES:(p + 1) * LANES] = pair.T.astype(BF16)

    _attention_pipeline(
        n_units, vt_ref.shape[2], [slice(0, 2 * n_pairs * tq)],
        lambda u: jnp.concatenate([_split_halves_t(qt_ref[u // n_q, p, u % n_q]) for p in range(n_pairs)], axis=1),
        lambda u: [(k_ref.at[u // n_q, 0], vt_ref.at[u // n_q, 0])],
        finalize, q_scr, s_bufs, p_bufs, acc_ref)


def _gqa_attn_call(qt, k, vt):
    per_kv = qt.shape[1] // GQA_KV_HEADS
    return _attn_call("gqa_attn", _gqa_attn_kernel, qt, k, vt, [], 2 * per_kv, per_kv, per_kv * LANES)


def _diff_attn_kernel(qt_ref, k_ref, vt_ref, lq1_ref, lk1_ref, lq2_ref, lk2_ref, sg_ref, o_ref, q_scr, s_bufs, p_bufs,
                      acc_ref, *, n_units, n_q, lam_init):
    tq = qt_ref.shape[4]

    def finalize(u, acc):
        o = _normalised(acc, 2 * DIFF_DIM)
        lam = (jnp.exp(jnp.sum(lq1_ref[...] * lk1_ref[...], axis=-1, keepdims=True))
               - jnp.exp(jnp.sum(lq2_ref[...] * lk2_ref[...], axis=-1, keepdims=True)) + lam_init)
        od = o[:, :tq] - lam * o[:, tq:]
        od = od * lax.rsqrt(jnp.mean(od * od, axis=0, keepdims=True) + EPS)
        o_ref[u // n_q, pl.ds(pl.multiple_of((u % n_q) * tq, tq), tq), :] = (
            ((od * sg_ref[...]) * (1.0 - lam_init)).T.astype(BF16))

    _attention_pipeline(
        n_units, vt_ref.shape[2], [slice(0, 2 * tq)],
        lambda u: _split_halves_t(qt_ref[u // n_q, 0, u % n_q]),
        lambda u: [(k_ref.at[u // n_q, 0], vt_ref.at[u // n_q, 0])],
        finalize, q_scr, s_bufs, p_bufs, acc_ref)


def _diff_attn_call(qt, k, vt, lq1, lk1, lq2, lk2, subln_g_col, lam_init):
    return _attn_call("diff_attn", functools.partial(_diff_attn_kernel, lam_init=lam_init), qt, k, vt,
                      [lq1, lk1, lq2, lk2, subln_g_col], 2, 1, LANES)


def _pad_cols(w, n):
    return jnp.pad(w, ((0, 0), (0, n)))


def _prep_even(ev_w_in, mla_w_uq, mla_w_ukv):
    o_kpe = MLA_Q_RANK + MLA_KV_RANK
    wz = jnp.concatenate([
        ev_w_in[:, :o_kpe],
        jnp.zeros((D_MODEL, MLA_NOPE), F32), ev_w_in[:, o_kpe:o_kpe + MLA_ROPE],
        jnp.zeros((D_MODEL, LANES - MLA_NOPE - MLA_ROPE), F32),
        ev_w_in[:, o_kpe + MLA_ROPE:]], axis=1)
    wuq = mla_w_uq.reshape(MLA_Q_RANK, MLA_HEADS, MLA_NOPE + MLA_ROPE)
    wuq = jnp.pad(wuq, ((0, 0), (0, 0), (0, LANES - MLA_NOPE - MLA_ROPE))).reshape(MLA_Q_RANK, MLA_HEADS * LANES)
    wukv = mla_w_ukv.reshape(MLA_KV_RANK, MLA_HEADS, MLA_NOPE + MLA_V)
    wuk = jnp.pad(wukv[:, :, :MLA_NOPE], ((0, 0), (0, 0), (0, LANES - MLA_NOPE)))
    wkv = jnp.concatenate([wuk.reshape(MLA_KV_RANK, MLA_HEADS * LANES),
                           wukv[:, :, MLA_NOPE:].reshape(MLA_KV_RANK, MLA_HEADS * MLA_V)], axis=1)
    return wz.astype(BF16), wuq.astype(BF16), wkv.astype(BF16)


def _prep_odd(od_w_in):
    n_qc = GQA_HEADS * GQA_DIM
    n_kv = GQA_KV_HEADS * GQA_DIM

    def dup(w):
        w = w.reshape(D_MODEL, GQA_KV_HEADS, 1, GQA_DIM)
        return jnp.broadcast_to(w, (D_MODEL, GQA_KV_HEADS, 2, GQA_DIM)).reshape(D_MODEL, 2 * n_kv)

    wz = jnp.concatenate([od_w_in[:, :n_qc], dup(od_w_in[:, n_qc:n_qc + n_kv]), od_w_in[:, n_qc + n_kv:]], axis=1)
    return wz.astype(BF16)


def _rope_tables(seq):
    pos = jnp.arange(seq)

    def angles(p, n_freq, theta):
        inv = theta ** (-jnp.arange(0, n_freq, dtype=F32) * 2.0 / (2 * n_freq))
        ang = p.astype(F32)[:, None] * inv[None, :]
        return jnp.cos(ang), jnp.sin(ang)

    one = lambda n: jnp.ones((seq, n), F32)
    zero = lambda n: jnp.zeros((seq, n), F32)

    def stack(c, s1, s2):
        reps = LANES // c.shape[1]
        return jnp.stack([jnp.tile(t, (1, reps)) for t in (c, s1, s2)])

    cos, sin = angles(pos, MLA_ROPE // 2, ROPE_THETA)
    pad = LANES - MLA_NOPE - MLA_ROPE
    mla = stack(jnp.concatenate([one(MLA_NOPE), cos, cos, one(pad)], 1),
                jnp.concatenate([zero(MLA_NOPE), -sin, zero(MLA_ROPE // 2), zero(pad)], 1),
                jnp.concatenate([zero(MLA_NOPE), zero(MLA_ROPE // 2), sin, zero(pad)], 1))
    q = GQA_DIM // 4
    cr, sr = angles(pos // GRID_W, q, AXIAL_THETA)
    cc, sc = angles(pos % GRID_W, q, AXIAL_THETA)
    axial = stack(jnp.concatenate([cr, cr, cc, cc], 1),
                  jnp.concatenate([-sr, zero(q), -sc, zero(q)], 1),
                  jnp.concatenate([zero(q), sr, zero(q), sc], 1))
    hr = DIFF_ROPE // 2
    cd, sd = angles(pos, hr, ROPE_THETA)
    rest = DIFF_DIM - DIFF_ROPE
    diff = stack(jnp.concatenate([cd, cd, one(rest)], 1),
                 jnp.concatenate([-sd, zero(hr), zero(rest)], 1),
                 jnp.concatenate([zero(hr), sd, zero(rest)], 1))
    return mla, axial, diff


def kernel(x_prompt, x_sample, c_prompt, c_sample, w_ada, b_ada, norm_g, ffn_w_in, ffn_w_out, ev_w_in, mla_gq, mla_w_uq, mla_gkv, mla_w_ukv, pool_w, pool_scale, ev_w_out, od_w_in, gqa_gq, gqa_gk, diff_lq1, diff_lk1, diff_lq2, diff_lk2, diff_subln_g, od_w_out, final_g):
    groups = [x_prompt, x_sample]
    n_rows = [x.shape[0] for x in groups]
    c_all = jnp.concatenate([c_prompt, c_sample], axis=0)
    c_all = jnp.pad(c_all, ((0, -c_all.shape[0] % 16), (0, 0)))
    mod = _ada_call(c_all, w_ada, b_ada)
    mods, r0 = [], 0
    for n in n_rows:
        mods.append(mod[:, r0:r0 + n].reshape(DEPTH, n, 1, 9 * D_MODEL))
        r0 += n

    ffn_in = ffn_w_in.astype(BF16)
    ffn_out = ffn_w_out.astype(BF16)
    row = lambda v: v.reshape(1, -1)
    tables = [_rope_tables(x.shape[1]) for x in groups]

    xs = list(groups)
    for i in range(DEPTH):
        j = i // 2
        if i % 2 == 0:
            wz, wuq, wkv = _prep_even(ev_w_in[j], mla_w_uq[j], mla_w_ukv[j])
            w_mix = ev_w_out[j].astype(BF16)
            poolw = pool_w[j].astype(BF16)
        else:
            wz = _prep_odd(od_w_in[j])
            w_mix = od_w_out[j].astype(BF16)
            gq2 = row(jnp.tile(gqa_gq[j], 2))
            gk2 = row(jnp.tile(gqa_gk[j], 2))
            lam_init = 0.8 - 0.6 * math.exp(-0.3 * i)
        for gi in range(len(groups)):
            x, md = xs[gi], mods[gi]
            tab_mla, tab_axial, tab_diff = tables[gi]
            x = _ffn_call(x, md, i, 0, row(norm_g[i, 0]), ffn_in[i, 0], ffn_out[i, 0])
            if i % 2 == 0:
                q, k, v, ob = _even_in_call(x, md, i, row(norm_g[i, 1]), wz, row(mla_gq[j]), wuq, row(mla_gkv[j]),
                                            wkv, tab_mla, poolw, row(pool_scale[j]))
                mix = (_mla_attn_call(q, k, v), ob, w_mix)
            else:
                qc, kc, vc, qd, kd, vd = _odd_in_call(x, md, i, row(norm_g[i, 1]), wz, gq2, gk2, tab_axial, tab_diff)
                oc = _gqa_attn_call(qc, kc, vc)
                od = _diff_attn_call(qd, kd, vd, row(diff_lq1[j]), row(diff_lk1[j]), row(diff_lq2[j]),
                                     row(diff_lk2[j]), diff_subln_g[j].reshape(-1, 1), lam_init)
                mix = (oc, od, w_mix)
            xs[gi] = _ffn_call(x, md, i, 1, row(norm_g[i, 2]), ffn_in[i, 1], ffn_out[i, 1], mix=mix,
                               final_g=row(final_g) if i == DEPTH - 1 else None)
    return tuple(xs)
```

```python
import functools
import math

import jax
import jax.numpy as jnp
from jax import lax
from jax.experimental import pallas as pl
from jax.experimental.pallas import tpu as pltpu

F32 = jnp.float32
BF16 = jnp.bfloat16

D_MODEL = 1024
DEPTH = 4
GRID_W = 64
EPS = 1e-6
D_FF = 2816
ROPE_THETA = 500000.0
AXIAL_THETA = 10000.0
MLA_HEADS = 8
MLA_NOPE = 64
MLA_ROPE = 32
MLA_V = 64
MLA_Q_RANK = 384
MLA_KV_RANK = 256
POOL_WINDOWS = (2, 4, 8, 16)
POOL_GROUP = 128
POOL_WIDTH = POOL_GROUP * len(POOL_WINDOWS)
POOL_HALO = 8
GQA_HEADS = 8
GQA_KV_HEADS = 2
GQA_DIM = 64
DIFF_HEADS = 4
DIFF_DIM = 64
DIFF_ROPE = DIFF_DIM // 4
LANES = 128
LOG2E = math.log2(math.e)

TOKEN_TILE = 512
FFN_TILE = 1024
KV_CHUNK = 512
SCORE_LANES = 2048
ATTN_KEYS_PER_STEP = 8192
EXP_ROWS = 16
DENOM_ROWS = 16
MLA_QSCALE = (MLA_NOPE + MLA_ROPE) ** -0.5 * LOG2E
GQA_QSCALE = GQA_DIM ** -0.5 * LOG2E
DIFF_QSCALE = DIFF_DIM ** -0.5 * LOG2E
ADA_COL_TILE = 1536
FFN_CHUNKS = ((0, 1024), (1024, 1024), (2048, 768))
VMEM_LIMIT = 56 * 1024 * 1024


def _params(n_grid):
    return pltpu.CompilerParams(dimension_semantics=("arbitrary",) * n_grid, vmem_limit_bytes=VMEM_LIMIT)


def _const_spec(shape):
    nd = len(shape)
    return pl.BlockSpec(shape, lambda *_: (0,) * nd, pipeline_mode=pl.Buffered(1))


def _mod_spec(layer, chunk):
    return pl.BlockSpec((None, None, 1, D_MODEL), lambda b, *_: (layer, b, 0, chunk))


def _dot(a, b):
    return jnp.dot(a, b, preferred_element_type=F32)


def _rms(x):
    return x * lax.rsqrt(jnp.mean(x * x, axis=-1, keepdims=True) + EPS)


def _modnorm(x, g, scale, shift):
    return (_rms(x) * g) * (1.0 + scale) + shift


def _silu(x):
    return x * jax.nn.sigmoid(x)


def _rope(x, tab_ref, half):
    width = x.shape[-1]
    fwd = pltpu.roll(x, width - half, axis=1)
    back = pltpu.roll(x, half, axis=1)
    return x * tab_ref[0] + fwd * tab_ref[1] + back * tab_ref[2]


def _store_value_chunks(vt_ref, v, rows):
    tm = v.shape[0]
    vt = jnp.concatenate([v[:, c0:c0 + LANES].T for c0 in range(0, v.shape[1], LANES)], axis=0)
    ones = (lax.broadcasted_iota(jnp.int32, (DENOM_ROWS, tm), 0) == 0).astype(F32)
    for blk in range(v.shape[1] // rows):
        ext = jnp.concatenate([vt[blk * rows:(blk + 1) * rows], ones], axis=0).astype(BF16)
        vt_ref[0, blk, 0] = ext


def _query_tile(seq, n_streams):
    tq = min(SCORE_LANES // n_streams, seq)
    assert tq % TOKEN_TILE == 0 and seq % tq == 0
    return tq


def _query_block_out(bsz, seq, n_blk, tq):
    per_blk = tq // TOKEN_TILE
    spec = pl.BlockSpec((1, n_blk, 1, LANES, TOKEN_TILE), lambda b, t: (b, 0, t // per_blk, 0, t % per_blk))
    return spec, jax.ShapeDtypeStruct((bsz, n_blk, seq // tq, LANES, tq), BF16)


def _value_chunk_out(bsz, seq, n_blk, rows):
    per_chunk = KV_CHUNK // TOKEN_TILE
    spec = pl.BlockSpec((1, n_blk, 1, rows + DENOM_ROWS, TOKEN_TILE),
                        lambda b, t: (b, 0, t // per_chunk, 0, t % per_chunk))
    return spec, jax.ShapeDtypeStruct((bsz, n_blk, seq // KV_CHUNK, rows + DENOM_ROWS, KV_CHUNK), BF16)


def _ada_kernel(c_ref, w_ref, b_ref, o_ref):
    sc = _silu(c_ref[...]).astype(BF16)
    o_ref[0] = _dot(sc, w_ref[0].astype(BF16)) + b_ref[0]


def _ada_call(c_all, w_ada, b_ada):
    rows = c_all.shape[0]
    n_out = w_ada.shape[-1]
    return pl.pallas_call(
        _ada_kernel,
        grid=(DEPTH, n_out // ADA_COL_TILE),
        in_specs=[
            pl.BlockSpec((rows, D_MODEL), lambda i, n: (0, 0)),
            pl.BlockSpec((1, D_MODEL, ADA_COL_TILE), lambda i, n: (i, 0, n)),
            pl.BlockSpec((1, 1, ADA_COL_TILE), lambda i, n: (i, 0, n)),
        ],
        out_specs=pl.BlockSpec((1, rows, ADA_COL_TILE), lambda i, n: (i, 0, n)),
        out_shape=jax.ShapeDtypeStruct((DEPTH, rows, n_out), F32),
        compiler_params=_params(2),
        name="ada_mod",
    )(c_all, w_ada, b_ada.reshape(DEPTH, 1, n_out))


def _ffn_kernel(*refs, has_mix, has_final):
    it = iter(refs)
    x_ref = next(it)
    if has_mix:
        ma_ref, mb_ref, wmix_ref, g1_ref = next(it), next(it), next(it), next(it)
    sh_ref, sc_ref, gt_ref, ng_ref, win_ref, wout_ref = (next(it) for _ in range(6))
    if has_final:
        fg_ref = next(it)
    o_ref = next(it)

    x = x_ref[0]
    if has_mix:
        half = wmix_ref.shape[0] // 2
        m = _dot(ma_ref[0], wmix_ref[:half, :]) + _dot(mb_ref[0], wmix_ref[half:, :])
        x = x + g1_ref[...] * m
    h = _modnorm(x, ng_ref[...], sc_ref[...], sh_ref[...]).astype(BF16)
    acc = None
    for c0, cn in FFN_CHUNKS:
        g = _dot(h, win_ref[:, c0:c0 + cn])
        u = _dot(h, win_ref[:, D_FF + c0:D_FF + c0 + cn])
        a = (_silu(g) * u).astype(BF16)
        d = _dot(a, wout_ref[c0:c0 + cn, :])
        acc = d if acc is None else acc + d
    y = x + (0.5 * gt_ref[...]) * acc
    if has_final:
        y = _rms(y) * fg_ref[...]
    o_ref[0] = y


def _ffn_call(x, mod, layer, which, norm_g_row, w_in, w_out, mix=None, final_g=None):
    bsz, seq, _ = x.shape
    tm = FFN_TILE
    base = 0 if which == 0 else 6
    tok = pl.BlockSpec((1, tm, D_MODEL), lambda b, t: (b, t, 0))
    args, specs = [x], [tok]
    if mix is not None:
        ma, mb, wmix = mix
        half = pl.BlockSpec((1, tm, D_MODEL // 2), lambda b, t: (b, t, 0))
        args += [ma, mb, wmix, mod]
        specs += [half, half, _const_spec(wmix.shape), _mod_spec(layer, 5)]
    args += [mod, mod, mod, norm_g_row, w_in, w_out]
    specs += [_mod_spec(layer, base), _mod_spec(layer, base + 1), _mod_spec(layer, base + 2),
              _const_spec(norm_g_row.shape), _const_spec(w_in.shape), _const_spec(w_out.shape)]
    if final_g is not None:
        args.append(final_g)
        specs.append(_const_spec(final_g.shape))
    return pl.pallas_call(
        functools.partial(_ffn_kernel, has_mix=mix is not None, has_final=final_g is not None),
        grid=(bsz, seq // tm),
        in_specs=specs,
        out_specs=tok,
        out_shape=jax.ShapeDtypeStruct(x.shape, F32),
        compiler_params=_params(2),
        name="ffn",
    )(*args)


def _pool(pz, halo, t_idx, n_tiles, seq, poolw_ref, pscale_ref):
    tm = pz.shape[0]
    prev = jnp.where(t_idx > 0, halo[:POOL_HALO], 0.0)
    nxt = jnp.where(t_idx < n_tiles - 1, halo[POOL_HALO:], 0.0)
    ext = jnp.concatenate([prev, pz, nxt], axis=0)
    n_ext = ext.shape[0]
    pos = t_idx * tm + lax.broadcasted_iota(jnp.int32, (tm, 1), 0)
    outs = []
    for g, w in enumerate(POOL_WINDOWS):
        sl = slice(g * POOL_GROUP, (g + 1) * POOL_GROUP)
        run = ext[:, sl]
        span = 1
        while span < w:
            run = run + pltpu.roll(run, n_ext - span, axis=0)
            span *= 2
        start = POOL_HALO - w // 2
        if start:
            run = pltpu.roll(run, n_ext - start, axis=0)
        win = run[:tm]
        cnt = jnp.minimum(pos + (w - w // 2), seq) - jnp.maximum(pos - w // 2, 0)
        d = (win / cnt.astype(F32) - pz[:, sl]).astype(BF16)
        outs.append(_dot(d, poolw_ref[g]))
    return jnp.concatenate(outs, axis=-1) * pscale_ref[...]


def _even_in_kernel(x_ref, xp_ref, xn_ref, sh_ref, sc_ref, ng_ref, wz_ref, gq_ref, wuq_ref, gkv_ref, wkv_ref,
                    tab_ref, poolw_ref, pscale_ref, q_ref, k_ref, v_ref, ob_ref, *, seq):
    t_idx = pl.program_id(1)
    n_tiles = pl.num_programs(1)
    g, sc, sh = ng_ref[...], sc_ref[...], sh_ref[...]
    h = _modnorm(x_ref[0], g, sc, sh).astype(BF16)
    z = _dot(h, wz_ref[...])
    o_kpe = MLA_Q_RANK + MLA_KV_RANK
    o_pz = o_kpe + LANES

    xh = jnp.concatenate([xp_ref[0], xn_ref[0]], axis=0)
    hh = _modnorm(xh, g, sc, sh).astype(BF16)
    halo = _dot(hh, wz_ref[:, o_pz:])
    ob_ref[0] = _pool(z[:, o_pz:], halo, t_idx, n_tiles, seq, poolw_ref, pscale_ref).astype(BF16)

    cq = (_rms(z[:, :MLA_Q_RANK]) * gq_ref[...]).astype(BF16)
    ckv = (_rms(z[:, MLA_Q_RANK:o_kpe]) * gkv_ref[...]).astype(BF16)
    q_all = _dot(cq, wuq_ref[...])
    kv_all = _dot(ckv, wkv_ref[...])
    kpe = _rope(z[:, o_kpe:o_pz], tab_ref, MLA_ROPE // 2)
    for hd in range(MLA_HEADS):
        sl = slice(hd * LANES, (hd + 1) * LANES)
        q_ref[0, hd, 0] = (_rope(q_all[:, sl], tab_ref, MLA_ROPE // 2) * MLA_QSCALE).T.astype(BF16)
        k_ref[0, hd] = (kv_all[:, sl] + kpe).astype(BF16)
    _store_value_chunks(v_ref, kv_all[:, MLA_HEADS * LANES:], MLA_V)


def _even_in_call(x, mod, layer, norm_g_row, wz, gq, wuq, gkv, wkv, tab, poolw, pscale):
    bsz, seq, _ = x.shape
    tm = TOKEN_TILE
    hb = tm // POOL_HALO
    n_hblk = seq // POOL_HALO
    tok = pl.BlockSpec((1, tm, D_MODEL), lambda b, t: (b, t, 0))
    prev = pl.BlockSpec((1, POOL_HALO, D_MODEL), lambda b, t: (b, jnp.maximum(t * hb - 1, 0), 0))
    nxt = pl.BlockSpec((1, POOL_HALO, D_MODEL), lambda b, t: (b, jnp.minimum((t + 1) * hb, n_hblk - 1), 0))
    heads = pl.BlockSpec((1, MLA_HEADS, tm, LANES), lambda b, t: (b, 0, t, 0))
    heads_t, out_heads_t = _query_block_out(bsz, seq, MLA_HEADS, _query_tile(seq, 2))
    half = pl.BlockSpec((1, tm, D_MODEL // 2), lambda b, t: (b, t, 0))
    out_heads = jax.ShapeDtypeStruct((bsz, MLA_HEADS, seq, LANES), BF16)
    out_half = jax.ShapeDtypeStruct((bsz, seq, D_MODEL // 2), BF16)
    vt_spec, vt_shape = _value_chunk_out(bsz, seq, MLA_HEADS, MLA_V)
    return pl.pallas_call(
        functools.partial(_even_in_kernel, seq=seq),
        grid=(bsz, seq // tm),
        in_specs=[tok, prev, nxt, _mod_spec(layer, 3), _mod_spec(layer, 4), _const_spec(norm_g_row.shape),
                  _const_spec(wz.shape), _const_spec(gq.shape), _const_spec(wuq.shape), _const_spec(gkv.shape),
                  _const_spec(wkv.shape), pl.BlockSpec((3, tm, LANES), lambda b, t: (0, t, 0)),
                  _const_spec(poolw.shape), _const_spec(pscale.shape)],
        out_specs=[heads_t, heads, vt_spec, half],
        out_shape=[out_heads_t, out_heads, vt_shape, out_half],
        compiler_params=_params(2),
        name="even_in",
    )(x, x, x, mod, mod, norm_g_row, wz, gq, wuq, gkv, wkv, tab, poolw, pscale)


def _pair_norm(x, gain):
    lo_mask = lax.broadcasted_iota(jnp.int32, x.shape, 1) < GQA_DIM
    sq = x * x
    lo = jnp.sum(jnp.where(lo_mask, sq, 0.0), axis=-1, keepdims=True)
    hi = jnp.sum(jnp.where(lo_mask, 0.0, sq), axis=-1, keepdims=True)
    inv = jnp.where(lo_mask, lax.rsqrt(lo / GQA_DIM + EPS), lax.rsqrt(hi / GQA_DIM + EPS))
    return x * inv * gain


def _odd_in_kernel(x_ref, sh_ref, sc_ref, ng_ref, wz_ref, gq_ref, gk_ref, tabc_ref, tabd_ref,
                   qc_ref, kc_ref, vc_ref, qd_ref, kd_ref, vd_ref):
    h = _modnorm(x_ref[0], ng_ref[...], sc_ref[...], sh_ref[...]).astype(BF16)
    z = _dot(h, wz_ref[...])
    n_qc = GQA_HEADS * GQA_DIM // LANES
    n_kv = GQA_KV_HEADS
    n_d = DIFF_HEADS
    col = 0
    for p in range(n_qc):
        blk = _rope(_pair_norm(z[:, col:col + LANES], gq_ref[...]), tabc_ref, GQA_DIM // 4)
        qc_ref[0, p, 0] = (blk * GQA_QSCALE).T.astype(BF16)
        col += LANES
    for j in range(n_kv):
        blk = z[:, col:col + LANES]
        kc_ref[0, j] = _rope(_pair_norm(blk, gk_ref[...]), tabc_ref, GQA_DIM // 4).astype(BF16)
        col += LANES
    _store_value_chunks(vc_ref, z[:, col:col + n_kv * GQA_DIM], GQA_DIM)
    col += n_kv * GQA_DIM
    for hd in range(n_d):
        qd_ref[0, hd, 0] = (_rope(z[:, col:col + LANES], tabd_ref, DIFF_ROPE // 2) * DIFF_QSCALE).T.astype(BF16)
        col += LANES
    for hd in range(n_d):
        kd_ref[0, hd] = _rope(z[:, col:col + LANES], tabd_ref, DIFF_ROPE // 2).astype(BF16)
        col += LANES
    _store_value_chunks(vd_ref, z[:, col:], 2 * DIFF_DIM)


def _odd_in_call(x, mod, layer, norm_g_row, wz, gq2, gk2, tabc, tabd):
    bsz, seq, _ = x.shape
    tm = TOKEN_TILE
    tok = pl.BlockSpec((1, tm, D_MODEL), lambda b, t: (b, t, 0))
    tab = pl.BlockSpec((3, tm, LANES), lambda b, t: (0, t, 0))

    def blocks(n):
        return (pl.BlockSpec((1, n, tm, LANES), lambda b, t: (b, 0, t, 0)),
                jax.ShapeDtypeStruct((bsz, n, seq, LANES), BF16))

    n_qc = GQA_HEADS * GQA_DIM // LANES
    outs = [_query_block_out(bsz, seq, n_qc, _query_tile(seq, 2 * n_qc // GQA_KV_HEADS)), blocks(GQA_KV_HEADS),
            _value_chunk_out(bsz, seq, GQA_KV_HEADS, GQA_DIM),
            _query_block_out(bsz, seq, DIFF_HEADS, _query_tile(seq, 2)), blocks(DIFF_HEADS),
            _value_chunk_out(bsz, seq, DIFF_HEADS, 2 * DIFF_DIM)]
    return pl.pallas_call(
        _odd_in_kernel,
        grid=(bsz, seq // tm),
        in_specs=[tok, _mod_spec(layer, 3), _mod_spec(layer, 4), _const_spec(norm_g_row.shape),
                  _const_spec(wz.shape), _const_spec(gq2.shape), _const_spec(gk2.shape), tab, tab],
        out_specs=[o[0] for o in outs],
        out_shape=[o[1] for o in outs],
        compiler_params=_params(2),
        name="odd_in",
    )(x, mod, mod, norm_g_row, wz, gq2, gk2, tabc, tabd)


def _attention_pipeline(n_units, n_chunks, lanes, load_q, stream_refs, finalize, q_scr, s_bufs, p_bufs, acc_ref):
    tk, n = s_bufs[0].shape
    assert n_chunks % 2 == 0 and n_chunks >= 2

    def cat(parts):
        return jnp.concatenate(parts, axis=1) if len(parts) > 1 else parts[0]

    def scores(u, j, s_ref):
        off = pl.multiple_of(j * tk, tk)
        cms = []
        for (k_ref, _), ln in zip(stream_refs(u), lanes):
            s = _dot(k_ref[pl.ds(off, tk), :], q_scr[:, ln])
            s_ref[:, ln] = s
            cms.append(jnp.max(s, axis=0, keepdims=True))
        return cat(cms)

    def values(u, j, p_ref, alpha):
        return cat([alpha[:, ln] * acc_ref[:, ln] + _dot(vt_ref[j], p_ref[:, ln])
                    for (_, vt_ref), ln in zip(stream_refs(u), lanes)])

    def exponentials(cur, m_new):
        for r in range(0, tk, EXP_ROWS):
            p_bufs[cur][r:r + EXP_ROWS, :] = jnp.exp2(s_bufs[cur][r:r + EXP_ROWS, :] - m_new).astype(BF16)

    def step(u, j, carry, cur, other):
        m, cm, alpha_prev = carry
        m_new = jnp.maximum(m, cm)
        alpha = jnp.exp2(m - m_new)
        cm_next = scores(u, j + 1, s_bufs[other])
        acc_ref[...] = values(u, j - 1, p_bufs[other], alpha_prev)
        exponentials(cur, m_new)
        return m_new, cm_next, alpha

    def first_step(u, cm):
        cm1 = scores(u, 1, s_bufs[1])
        acc_ref[...] = jnp.zeros(acc_ref.shape, F32)
        exponentials(0, cm)
        return cm, cm1, jnp.ones((1, n), F32)

    def unit(u, carry):
        def body(jj, carry):
            carry = step(u, 2 * jj + 1, carry, 1, 0)
            return step(u, 2 * jj + 2, carry, 0, 1)

        m, cm_last, alpha_prev = lax.fori_loop(0, (n_chunks - 2) // 2, body, carry)
        m_new = jnp.maximum(m, cm_last)
        alpha = jnp.exp2(m - m_new)
        nxt = jnp.minimum(u + 1, n_units - 1)
        q_scr[...] = load_q(nxt)
        cm_next = scores(nxt, 0, s_bufs[0])
        acc_ref[...] = values(u, n_chunks - 2, p_bufs[0], alpha_prev)
        exponentials(1, m_new)
        finalize(u, values(u, n_chunks - 1, p_bufs[1], alpha))
        return first_step(nxt, cm_next)

    q_scr[...] = load_q(0)
    lax.fori_loop(0, n_units, unit, first_step(0, scores(0, 0, s_bufs[0])))


def _normalised(acc, rows):
    return acc[:rows] * (1.0 / acc[rows:rows + 1])


def _row_lo(shape):
    return lax.broadcasted_iota(jnp.int32, shape, 0) < LANES // 2


def _split_halves_t(qp):
    lo = _row_lo(qp.shape)
    return jnp.concatenate([jnp.where(lo, qp, 0), jnp.where(lo, 0, qp)], axis=1).astype(BF16)


def _attn_call(name, kernel_fn, qt, k, vt, extra, heads_per_step, q_blocks_per_step, out_lanes):
    bsz, n_qblk, n_q, _, tq = qt.shape
    seq = k.shape[2]
    n_groups = n_qblk // q_blocks_per_step
    k_per, v_per = k.shape[1] // n_groups, vt.shape[1] // n_groups
    nb = max(1, min(bsz, ATTN_KEYS_PER_STEP // seq))
    assert bsz % nb == 0
    n_lanes = heads_per_step * tq
    rows = vt.shape[3]
    return pl.pallas_call(
        functools.partial(kernel_fn, n_units=nb * n_q, n_q=n_q),
        grid=(bsz // nb, n_groups),
        in_specs=[pl.BlockSpec((nb, q_blocks_per_step, n_q, LANES, tq), lambda b, g: (b, g, 0, 0, 0)),
                  pl.BlockSpec((nb, k_per, seq, LANES), lambda b, g: (b, g, 0, 0)),
                  pl.BlockSpec((nb, v_per) + vt.shape[2:], lambda b, g: (b, g, 0, 0, 0))]
        + [_const_spec(e.shape) for e in extra],
        out_specs=pl.BlockSpec((nb, seq, out_lanes), lambda b, g: (b, 0, g)),
        out_shape=jax.ShapeDtypeStruct((bsz, seq, D_MODEL // 2), BF16),
        scratch_shapes=[pltpu.VMEM((LANES, n_lanes), BF16), [pltpu.VMEM((KV_CHUNK, n_lanes), F32)] * 2,
                        [pltpu.VMEM((KV_CHUNK, n_lanes), BF16)] * 2, pltpu.VMEM((rows, n_lanes), F32)],
        compiler_params=_params(2),
        name=name,
    )(qt, k, vt, *extra)


def _mla_attn_kernel(qt_ref, k_ref, vt_ref, o_ref, q_scr, s_bufs, p_bufs, acc_ref, *, n_units, n_q):
    tq = qt_ref.shape[4]

    def finalize(u, acc):
        o = _normalised(acc, MLA_V)
        o_ref[u // n_q, pl.ds(pl.multiple_of((u % n_q) * tq, tq), tq), :] = (
            jnp.concatenate([o[:, :tq], o[:, tq:]], axis=0).T.astype(BF16))

    _attention_pipeline(
        n_units, vt_ref.shape[2], [slice(0, tq), slice(tq, 2 * tq)],
        lambda u: jnp.concatenate([qt_ref[u // n_q, h, u % n_q] for h in range(2)], axis=1),
        lambda u: [(k_ref.at[u // n_q, h], vt_ref.at[u // n_q, h]) for h in range(2)],
        finalize, q_scr, s_bufs, p_bufs, acc_ref)


def _mla_attn_call(qt, k, vt):
    return _attn_call("mla_attn", _mla_attn_kernel, qt, k, vt, [], 2, 2, LANES)


def _gqa_attn_kernel(qt_ref, k_ref, vt_ref, o_ref, q_scr, s_bufs, p_bufs, acc_ref, *, n_units, n_q):
    n_pairs, tq = qt_ref.shape[1], qt_ref.shape[4]

    def finalize(u, acc):
        o = _normalised(acc, GQA_DIM)
        rows = pl.ds(pl.multiple_of((u % n_q) * tq, tq), tq)
        for p in range(n_pairs):
            pair = jnp.concatenate([o[:, (2 * p) * tq:(2 * p + 1) * tq], o[:, (2 * p + 1) * tq:(2 * p + 2) * tq]],
                                   axis=0)
            o_ref[u // n_q, rows, p * LANES:(p + 1) * LANES] = pair.T.astype(BF16)

    _attention_pipeline(
        n_units, vt_ref.shape[2], [slice(0, 2 * n_pairs * tq)],
        lambda u: jnp.concatenate([_split_halves_t(qt_ref[u // n_q, p, u % n_q]) for p in range(n_pairs)], axis=1),
        lambda u: [(k_ref.at[u // n_q, 0], vt_ref.at[u // n_q, 0])],
        finalize, q_scr, s_bufs, p_bufs, acc_ref)


def _gqa_attn_call(qt, k, vt):
    per_kv = qt.shape[1] // GQA_KV_HEADS
    return _attn_call("gqa_attn", _gqa_attn_kernel, qt, k, vt, [], 2 * per_kv, per_kv, per_kv * LANES)


def _diff_attn_kernel(qt_ref, k_ref, vt_ref, lq1_ref, lk1_ref, lq2_ref, lk2_ref, sg_ref, o_ref, q_scr, s_bufs, p_bufs,
                      acc_ref, *, n_units, n_q, lam_init):
    tq = qt_ref.shape[4]

    def finalize(u, acc):
        o = _normalised(acc, 2 * DIFF_DIM)
        lam = (jnp.exp(jnp.sum(lq1_ref[...] * lk1_ref[...], axis=-1, keepdims=True))
               - jnp.exp(jnp.sum(lq2_ref[...] * lk2_ref[...], axis=-1, keepdims=True)) + lam_init)
        od = o[:, :tq] - lam * o[:, tq:]
        od = od * lax.rsqrt(jnp.mean(od * od, axis=0, keepdims=True) + EPS)
        o_ref[u // n_q, pl.ds(pl.multiple_of((u % n_q) * tq, tq), tq), :] = (
            ((od * sg_ref[...]) * (1.0 - lam_init)).T.astype(BF16))

    _attention_pipeline(
        n_units, vt_ref.shape[2], [slice(0, 2 * tq)],
        lambda u: _split_halves_t(qt_ref[u // n_q, 0, u % n_q]),
        lambda u: [(k_ref.at[u // n_q, 0], vt_ref.at[u // n_q, 0])],
        finalize, q_scr, s_bufs, p_bufs, acc_ref)


def _diff_attn_call(qt, k, vt, lq1, lk1, lq2, lk2, subln_g_col, lam_init):
    return _attn_call("diff_attn", functools.partial(_diff_attn_kernel, lam_init=lam_init), qt, k, vt,
                      [lq1, lk1, lq2, lk2, subln_g_col], 2, 1, LANES)


def _pad_cols(w, n):
    return jnp.pad(w, ((0, 0), (0, n)))


def _prep_even(ev_w_in, mla_w_uq, mla_w_ukv):
    o_kpe = MLA_Q_RANK + MLA_KV_RANK
    wz = jnp.concatenate([
        ev_w_in[:, :o_kpe],
        jnp.zeros((D_MODEL, MLA_NOPE), F32), ev_w_in[:, o_kpe:o_kpe + MLA_ROPE],
        jnp.zeros((D_MODEL, LANES - MLA_NOPE - MLA_ROPE), F32),
        ev_w_in[:, o_kpe + MLA_ROPE:]], axis=1)
    wuq = mla_w_uq.reshape(MLA_Q_RANK, MLA_HEADS, MLA_NOPE + MLA_ROPE)
    wuq = jnp.pad(wuq, ((0, 0), (0, 0), (0, LANES - MLA_NOPE - MLA_ROPE))).reshape(MLA_Q_RANK, MLA_HEADS * LANES)
    wukv = mla_w_ukv.reshape(MLA_KV_RANK, MLA_HEADS, MLA_NOPE + MLA_V)
    wuk = jnp.pad(wukv[:, :, :MLA_NOPE], ((0, 0), (0, 0), (0, LANES - MLA_NOPE)))
    wkv = jnp.concatenate([wuk.reshape(MLA_KV_RANK, MLA_HEADS * LANES),
                           wukv[:, :, MLA_NOPE:].reshape(MLA_KV_RANK, MLA_HEADS * MLA_V)], axis=1)
    return wz.astype(BF16), wuq.astype(BF16), wkv.astype(BF16)


def _prep_odd(od_w_in):
    n_qc = GQA_HEADS * GQA_DIM
    n_kv = GQA_KV_HEADS * GQA_DIM

    def dup(w):
        w = w.reshape(D_MODEL, GQA_KV_HEADS, 1, GQA_DIM)
        return jnp.broadcast_to(w, (D_MODEL, GQA_KV_HEADS, 2, GQA_DIM)).reshape(D_MODEL, 2 * n_kv)

    wz = jnp.concatenate([od_w_in[:, :n_qc], dup(od_w_in[:, n_qc:n_qc + n_kv]), od_w_in[:, n_qc + n_kv:]], axis=1)
    return wz.astype(BF16)


def _rope_tables(seq):
    pos = jnp.arange(seq)

    def angles(p, n_freq, theta):
        inv = theta ** (-jnp.arange(0, n_freq, dtype=F32) * 2.0 / (2 * n_freq))
        ang = p.astype(F32)[:, None] * inv[None, :]
        return jnp.cos(ang), jnp.sin(ang)

    one = lambda n: jnp.ones((seq, n), F32)
    zero = lambda n: jnp.zeros((seq, n), F32)

    def stack(c, s1, s2):
        reps = LANES // c.shape[1]
        return jnp.stack([jnp.tile(t, (1, reps)) for t in (c, s1, s2)])

    cos, sin = angles(pos, MLA_ROPE // 2, ROPE_THETA)
    pad = LANES - MLA_NOPE - MLA_ROPE
    mla = stack(jnp.concatenate([one(MLA_NOPE), cos, cos, one(pad)], 1),
                jnp.concatenate([zero(MLA_NOPE), -sin, zero(MLA_ROPE // 2), zero(pad)], 1),
                jnp.concatenate([zero(MLA_NOPE), zero(MLA_ROPE // 2), sin, zero(pad)], 1))
    q = GQA_DIM // 4
    cr, sr = angles(pos // GRID_W, q, AXIAL_THETA)
    cc, sc = angles(pos % GRID_W, q, AXIAL_THETA)
    axial = stack(jnp.concatenate([cr, cr, cc, cc], 1),
                  jnp.concatenate([-sr, zero(q), -sc, zero(q)], 1),
                  jnp.concatenate([zero(q), sr, zero(q), sc], 1))
    hr = DIFF_ROPE // 2
    cd, sd = angles(pos, hr, ROPE_THETA)
    rest = DIFF_DIM - DIFF_ROPE
    diff = stack(jnp.concatenate([cd, cd, one(rest)], 1),
                 jnp.concatenate([-sd, zero(hr), zero(rest)], 1),
                 jnp.concatenate([zero(hr), sd, zero(rest)], 1))
    return mla, axial, diff


def kernel(x_prompt, x_sample, c_prompt, c_sample, w_ada, b_ada, norm_g, ffn_w_in, ffn_w_out, ev_w_in, mla_gq, mla_w_uq, mla_gkv, mla_w_ukv, pool_w, pool_scale, ev_w_out, od_w_in, gqa_gq, gqa_gk, diff_lq1, diff_lk1, diff_lq2, diff_lk2, diff_subln_g, od_w_out, final_g):
    groups = [x_prompt, x_sample]
    n_rows = [x.shape[0] for x in groups]
    c_all = jnp.concatenate([c_prompt, c_sample], axis=0)
    c_all = jnp.pad(c_all, ((0, -c_all.shape[0] % 16), (0, 0)))
    mod = _ada_call(c_all, w_ada, b_ada)
    mods, r0 = [], 0
    for n in n_rows:
        mods.append(mod[:, r0:r0 + n].reshape(DEPTH, n, 1, 9 * D_MODEL))
        r0 += n

    ffn_in = ffn_w_in.astype(BF16)
    ffn_out = ffn_w_out.astype(BF16)
    row = lambda v: v.reshape(1, -1)
    tables = [_rope_tables(x.shape[1]) for x in groups]

    xs = list(groups)
    for i in range(DEPTH):
        j = i // 2
        if i % 2 == 0:
            wz, wuq, wkv = _prep_even(ev_w_in[j], mla_w_uq[j], mla_w_ukv[j])
            w_mix = ev_w_out[j].astype(BF16)
            poolw = pool_w[j].astype(BF16)
        else:
            wz = _prep_odd(od_w_in[j])
            w_mix = od_w_out[j].astype(BF16)
            gq2 = row(jnp.tile(gqa_gq[j], 2))
            gk2 = row(jnp.tile(gqa_gk[j], 2))
            lam_init = 0.8 - 0.6 * math.exp(-0.3 * i)
        for gi in range(len(groups)):
            x, md = xs[gi], mods[gi]
            tab_mla, tab_axial, tab_diff = tables[gi]
            x = _ffn_call(x, md, i, 0, row(norm_g[i, 0]), ffn_in[i, 0], ffn_out[i, 0])
            if i % 2 == 0:
                q, k, v, ob = _even_in_call(x, md, i, row(norm_g[i, 1]), wz, row(mla_gq[j]), wuq, row(mla_gkv[j]),
                                            wkv, tab_mla, poolw, row(pool_scale[j]))
                mix = (_mla_attn_call(q, k, v), ob, w_mix)
            else:
                qc, kc, vc, qd, kd, vd = _odd_in_call(x, md, i, row(norm_g[i, 1]), wz, gq2, gk2, tab_axial, tab_diff)
                oc = _gqa_attn_call(qc, kc, vc)
                od = _diff_attn_call(qd, kd, vd, row(diff_lq1[j]), row(diff_lk1[j]), row(diff_lq2[j]),
                                     row(diff_lk2[j]), diff_subln_g[j].reshape(-1, 1), lam_init)
                mix = (oc, od, w_mix)
            xs[gi] = _ffn_call(x, md, i, 1, row(norm_g[i, 2]), ffn_in[i, 1], ffn_out[i, 1], mix=mix,
                               final_g=row(final_g) if i == DEPTH - 1 else None)
    return tuple(xs)
```

```python
import functools
import math

import jax
import jax.numpy as jnp
from jax import lax
from jax.experimental import pallas as pl
from jax.experimental.pallas import tpu as pltpu

F32 = jnp.float32
BF16 = jnp.bfloat16

D_MODEL = 1024
DEPTH = 4
GRID_W = 64
EPS = 1e-6
D_FF = 2816
ROPE_THETA = 500000.0
AXIAL_THETA = 10000.0
MLA_HEADS = 8
MLA_NOPE = 64
MLA_ROPE = 32
MLA_V = 64
MLA_Q_RANK = 384
MLA_KV_RANK = 256
POOL_WINDOWS = (2, 4, 8, 16)
POOL_GROUP = 128
POOL_WIDTH = POOL_GROUP * len(POOL_WINDOWS)
POOL_HALO = 8
GQA_HEADS = 8
GQA_KV_HEADS = 2
GQA_DIM = 64
DIFF_HEADS = 4
DIFF_DIM = 64
DIFF_ROPE = DIFF_DIM // 4
LANES = 128
LOG2E = math.log2(math.e)

TOKEN_TILE = 512
FFN_TILE = 1024
KV_CHUNK_MIN, KV_CHUNK_MAX = 512, 1024
SCORE_LANES = 2048
ATTN_KEYS_PER_STEP = 8192
EXP_ROWS = 16
DENOM_ROWS = 16
MLA_QSCALE = (MLA_NOPE + MLA_ROPE) ** -0.5 * LOG2E
GQA_QSCALE = GQA_DIM ** -0.5 * LOG2E
DIFF_QSCALE = DIFF_DIM ** -0.5 * LOG2E
ADA_COL_TILE = 1536
FFN_CHUNKS = ((0, 1024), (1024, 1024), (2048, 768))
VMEM_LIMIT = 56 * 1024 * 1024


def _params(n_grid):
    return pltpu.CompilerParams(dimension_semantics=("arbitrary",) * n_grid, vmem_limit_bytes=VMEM_LIMIT)


def _const_spec(shape):
    nd = len(shape)
    return pl.BlockSpec(shape, lambda *_: (0,) * nd, pipeline_mode=pl.Buffered(1))


def _mod_spec(layer, chunk):
    return pl.BlockSpec((None, None, 1, D_MODEL), lambda b, *_: (layer, b, 0, chunk))


def _dot(a, b):
    return jnp.dot(a, b, preferred_element_type=F32)


def _rms(x):
    return x * lax.rsqrt(jnp.mean(x * x, axis=-1, keepdims=True) + EPS)


def _modnorm(x, g, scale, shift):
    return (_rms(x) * g) * (1.0 + scale) + shift


def _silu(x):
    return x * jax.nn.sigmoid(x)


def _rope(x, tab_ref, half):
    width = x.shape[-1]
    fwd = pltpu.roll(x, width - half, axis=1)
    back = pltpu.roll(x, half, axis=1)
    return x * tab_ref[0] + fwd * tab_ref[1] + back * tab_ref[2]


def _store_value_chunks(vt_ref, v, rows):
    tm = v.shape[0]
    vt = jnp.concatenate([v[:, c0:c0 + LANES].T for c0 in range(0, v.shape[1], LANES)], axis=0)
    ones = (lax.broadcasted_iota(jnp.int32, (DENOM_ROWS, tm), 0) == 0).astype(F32)
    for blk in range(v.shape[1] // rows):
        ext = jnp.concatenate([vt[blk * rows:(blk + 1) * rows], ones], axis=0).astype(BF16)
        vt_ref[0, blk, 0] = ext


def _query_tile(seq, n_streams):
    tq = min(SCORE_LANES // n_streams, seq)
    assert tq % TOKEN_TILE == 0 and seq % tq == 0
    return tq


def _query_block_out(bsz, seq, n_blk, tq):
    per_blk = tq // TOKEN_TILE
    spec = pl.BlockSpec((1, n_blk, 1, LANES, TOKEN_TILE), lambda b, t: (b, 0, t // per_blk, 0, t % per_blk))
    return spec, jax.ShapeDtypeStruct((bsz, n_blk, seq // tq, LANES, tq), BF16)


def _kv_chunk(seq):
    tk = max(KV_CHUNK_MIN, min(KV_CHUNK_MAX, seq // 8))
    assert seq % (2 * tk) == 0 and tk % TOKEN_TILE == 0
    return tk


def _value_chunk_out(bsz, seq, n_blk, rows):
    tk = _kv_chunk(seq)
    per_chunk = tk // TOKEN_TILE
    spec = pl.BlockSpec((1, n_blk, 1, rows + DENOM_ROWS, TOKEN_TILE),
                        lambda b, t: (b, 0, t // per_chunk, 0, t % per_chunk))
    return spec, jax.ShapeDtypeStruct((bsz, n_blk, seq // tk, rows + DENOM_ROWS, tk), BF16)


def _ada_kernel(c_ref, w_ref, b_ref, o_ref):
    sc = _silu(c_ref[...]).astype(BF16)
    o_ref[0] = _dot(sc, w_ref[0].astype(BF16)) + b_ref[0]


def _ada_call(c_all, w_ada, b_ada):
    rows = c_all.shape[0]
    n_out = w_ada.shape[-1]
    return pl.pallas_call(
        _ada_kernel,
        grid=(DEPTH, n_out // ADA_COL_TILE),
        in_specs=[
            pl.BlockSpec((rows, D_MODEL), lambda i, n: (0, 0)),
            pl.BlockSpec((1, D_MODEL, ADA_COL_TILE), lambda i, n: (i, 0, n)),
            pl.BlockSpec((1, 1, ADA_COL_TILE), lambda i, n: (i, 0, n)),
        ],
        out_specs=pl.BlockSpec((1, rows, ADA_COL_TILE), lambda i, n: (i, 0, n)),
        out_shape=jax.ShapeDtypeStruct((DEPTH, rows, n_out), F32),
        compiler_params=_params(2),
        name="ada_mod",
    )(c_all, w_ada, b_ada.reshape(DEPTH, 1, n_out))


def _ffn_kernel(*refs, has_mix, has_final):
    it = iter(refs)
    x_ref = next(it)
    if has_mix:
        ma_ref, mb_ref, wmix_ref, g1_ref = next(it), next(it), next(it), next(it)
    sh_ref, sc_ref, gt_ref, ng_ref, win_ref, wout_ref = (next(it) for _ in range(6))
    if has_final:
        fg_ref = next(it)
    o_ref = next(it)

    x = x_ref[0]
    if has_mix:
        half = wmix_ref.shape[0] // 2
        m = _dot(ma_ref[0], wmix_ref[:half, :]) + _dot(mb_ref[0], wmix_ref[half:, :])
        x = x + g1_ref[...] * m
    h = _modnorm(x, ng_ref[...], sc_ref[...], sh_ref[...]).astype(BF16)
    acc = None
    for c0, cn in FFN_CHUNKS:
        g = _dot(h, win_ref[:, c0:c0 + cn])
        u = _dot(h, win_ref[:, D_FF + c0:D_FF + c0 + cn])
        a = (_silu(g) * u).astype(BF16)
        d = _dot(a, wout_ref[c0:c0 + cn, :])
        acc = d if acc is None else acc + d
    y = x + (0.5 * gt_ref[...]) * acc
    if has_final:
        y = _rms(y) * fg_ref[...]
    o_ref[0] = y


def _ffn_call(x, mod, layer, which, norm_g_row, w_in, w_out, mix=None, final_g=None):
    bsz, seq, _ = x.shape
    tm = FFN_TILE
    base = 0 if which == 0 else 6
    tok = pl.BlockSpec((1, tm, D_MODEL), lambda b, t: (b, t, 0))
    args, specs = [x], [tok]
    if mix is not None:
        ma, mb, wmix = mix
        half = pl.BlockSpec((1, tm, D_MODEL // 2), lambda b, t: (b, t, 0))
        args += [ma, mb, wmix, mod]
        specs += [half, half, _const_spec(wmix.shape), _mod_spec(layer, 5)]
    args += [mod, mod, mod, norm_g_row, w_in, w_out]
    specs += [_mod_spec(layer, base), _mod_spec(layer, base + 1), _mod_spec(layer, base + 2),
              _const_spec(norm_g_row.shape), _const_spec(w_in.shape), _const_spec(w_out.shape)]
    if final_g is not None:
        args.append(final_g)
        specs.append(_const_spec(final_g.shape))
    return pl.pallas_call(
        functools.partial(_ffn_kernel, has_mix=mix is not None, has_final=final_g is not None),
        grid=(bsz, seq // tm),
        in_specs=specs,
        out_specs=tok,
        out_shape=jax.ShapeDtypeStruct(x.shape, F32),
        compiler_params=_params(2),
        name="ffn",
    )(*args)


def _pool(pz, halo, t_idx, n_tiles, seq, poolw_ref, pscale_ref):
    tm = pz.shape[0]
    prev = jnp.where(t_idx > 0, halo[:POOL_HALO], 0.0)
    nxt = jnp.where(t_idx < n_tiles - 1, halo[POOL_HALO:], 0.0)
    ext = jnp.concatenate([prev, pz, nxt], axis=0)
    n_ext = ext.shape[0]
    pos = t_idx * tm + lax.broadcasted_iota(jnp.int32, (tm, 1), 0)
    outs = []
    for g, w in enumerate(POOL_WINDOWS):
        sl = slice(g * POOL_GROUP, (g + 1) * POOL_GROUP)
        run = ext[:, sl]
        span = 1
        while span < w:
            run = run + pltpu.roll(run, n_ext - span, axis=0)
            span *= 2
        start = POOL_HALO - w // 2
        if start:
            run = pltpu.roll(run, n_ext - start, axis=0)
        win = run[:tm]
        cnt = jnp.minimum(pos + (w - w // 2), seq) - jnp.maximum(pos - w // 2, 0)
        d = (win / cnt.astype(F32) - pz[:, sl]).astype(BF16)
        outs.append(_dot(d, poolw_ref[g]))
    return jnp.concatenate(outs, axis=-1) * pscale_ref[...]


def _even_in_kernel(x_ref, xp_ref, xn_ref, sh_ref, sc_ref, ng_ref, wz_ref, gq_ref, wuq_ref, gkv_ref, wkv_ref,
                    tab_ref, poolw_ref, pscale_ref, q_ref, k_ref, v_ref, ob_ref, *, seq):
    t_idx = pl.program_id(1)
    n_tiles = pl.num_programs(1)
    g, sc, sh = ng_ref[...], sc_ref[...], sh_ref[...]
    h = _modnorm(x_ref[0], g, sc, sh).astype(BF16)
    z = _dot(h, wz_ref[...])
    o_kpe = MLA_Q_RANK + MLA_KV_RANK
    o_pz = o_kpe + LANES

    xh = jnp.concatenate([xp_ref[0], xn_ref[0]], axis=0)
    hh = _modnorm(xh, g, sc, sh).astype(BF16)
    halo = _dot(hh, wz_ref[:, o_pz:])
    ob_ref[0] = _pool(z[:, o_pz:], halo, t_idx, n_tiles, seq, poolw_ref, pscale_ref).astype(BF16)

    cq = (_rms(z[:, :MLA_Q_RANK]) * gq_ref[...]).astype(BF16)
    ckv = (_rms(z[:, MLA_Q_RANK:o_kpe]) * gkv_ref[...]).astype(BF16)
    q_all = _dot(cq, wuq_ref[...])
    kv_all = _dot(ckv, wkv_ref[...])
    kpe = _rope(z[:, o_kpe:o_pz], tab_ref, MLA_ROPE // 2)
    for hd in range(MLA_HEADS):
        sl = slice(hd * LANES, (hd + 1) * LANES)
        q_ref[0, hd, 0] = (_rope(q_all[:, sl], tab_ref, MLA_ROPE // 2) * MLA_QSCALE).T.astype(BF16)
        k_ref[0, hd] = (kv_all[:, sl] + kpe).astype(BF16)
    _store_value_chunks(v_ref, kv_all[:, MLA_HEADS * LANES:], MLA_V)


def _even_in_call(x, mod, layer, norm_g_row, wz, gq, wuq, gkv, wkv, tab, poolw, pscale):
    bsz, seq, _ = x.shape
    tm = TOKEN_TILE
    hb = tm // POOL_HALO
    n_hblk = seq // POOL_HALO
    tok = pl.BlockSpec((1, tm, D_MODEL), lambda b, t: (b, t, 0))
    prev = pl.BlockSpec((1, POOL_HALO, D_MODEL), lambda b, t: (b, jnp.maximum(t * hb - 1, 0), 0))
    nxt = pl.BlockSpec((1, POOL_HALO, D_MODEL), lambda b, t: (b, jnp.minimum((t + 1) * hb, n_hblk - 1), 0))
    heads = pl.BlockSpec((1, MLA_HEADS, tm, LANES), lambda b, t: (b, 0, t, 0))
    heads_t, out_heads_t = _query_block_out(bsz, seq, MLA_HEADS, _query_tile(seq, 2))
    half = pl.BlockSpec((1, tm, D_MODEL // 2), lambda b, t: (b, t, 0))
    out_heads = jax.ShapeDtypeStruct((bsz, MLA_HEADS, seq, LANES), BF16)
    out_half = jax.ShapeDtypeStruct((bsz, seq, D_MODEL // 2), BF16)
    vt_spec, vt_shape = _value_chunk_out(bsz, seq, MLA_HEADS, MLA_V)
    return pl.pallas_call(
        functools.partial(_even_in_kernel, seq=seq),
        grid=(bsz, seq // tm),
        in_specs=[tok, prev, nxt, _mod_spec(layer, 3), _mod_spec(layer, 4), _const_spec(norm_g_row.shape),
                  _const_spec(wz.shape), _const_spec(gq.shape), _const_spec(wuq.shape), _const_spec(gkv.shape),
                  _const_spec(wkv.shape), pl.BlockSpec((3, tm, LANES), lambda b, t: (0, t, 0)),
                  _const_spec(poolw.shape), _const_spec(pscale.shape)],
        out_specs=[heads_t, heads, vt_spec, half],
        out_shape=[out_heads_t, out_heads, vt_shape, out_half],
        compiler_params=_params(2),
        name="even_in",
    )(x, x, x, mod, mod, norm_g_row, wz, gq, wuq, gkv, wkv, tab, poolw, pscale)


def _pair_norm(x, gain):
    lo_mask = lax.broadcasted_iota(jnp.int32, x.shape, 1) < GQA_DIM
    sq = x * x
    lo = jnp.sum(jnp.where(lo_mask, sq, 0.0), axis=-1, keepdims=True)
    hi = jnp.sum(jnp.where(lo_mask, 0.0, sq), axis=-1, keepdims=True)
    inv = jnp.where(lo_mask, lax.rsqrt(lo / GQA_DIM + EPS), lax.rsqrt(hi / GQA_DIM + EPS))
    return x * inv * gain


def _odd_in_kernel(x_ref, sh_ref, sc_ref, ng_ref, wz_ref, gq_ref, gk_ref, tabc_ref, tabd_ref,
                   qc_ref, kc_ref, vc_ref, qd_ref, kd_ref, vd_ref):
    h = _modnorm(x_ref[0], ng_ref[...], sc_ref[...], sh_ref[...]).astype(BF16)
    z = _dot(h, wz_ref[...])
    n_qc = GQA_HEADS * GQA_DIM // LANES
    n_kv = GQA_KV_HEADS
    n_d = DIFF_HEADS
    col = 0
    for p in range(n_qc):
        blk = _rope(_pair_norm(z[:, col:col + LANES], gq_ref[...]), tabc_ref, GQA_DIM // 4)
        qc_ref[0, p, 0] = (blk * GQA_QSCALE).T.astype(BF16)
        col += LANES
    for j in range(n_kv):
        blk = z[:, col:col + LANES]
        kc_ref[0, j] = _rope(_pair_norm(blk, gk_ref[...]), tabc_ref, GQA_DIM // 4).astype(BF16)
        col += LANES
    _store_value_chunks(vc_ref, z[:, col:col + n_kv * GQA_DIM], GQA_DIM)
    col += n_kv * GQA_DIM
    for hd in range(n_d):
        qd_ref[0, hd, 0] = (_rope(z[:, col:col + LANES], tabd_ref, DIFF_ROPE // 2) * DIFF_QSCALE).T.astype(BF16)
        col += LANES
    for hd in range(n_d):
        kd_ref[0, hd] = _rope(z[:, col:col + LANES], tabd_ref, DIFF_ROPE // 2).astype(BF16)
        col += LANES
    _store_value_chunks(vd_ref, z[:, col:], 2 * DIFF_DIM)


def _odd_in_call(x, mod, layer, norm_g_row, wz, gq2, gk2, tabc, tabd):
    bsz, seq, _ = x.shape
    tm = TOKEN_TILE
    tok = pl.BlockSpec((1, tm, D_MODEL), lambda b, t: (b, t, 0))
    tab = pl.BlockSpec((3, tm, LANES), lambda b, t: (0, t, 0))

    def blocks(n):
        return (pl.BlockSpec((1, n, tm, LANES), lambda b, t: (b, 0, t, 0)),
                jax.ShapeDtypeStruct((bsz, n, seq, LANES), BF16))

    n_qc = GQA_HEADS * GQA_DIM // LANES
    outs = [_query_block_out(bsz, seq, n_qc, _query_tile(seq, 2 * n_qc // GQA_KV_HEADS)), blocks(GQA_KV_HEADS),
            _value_chunk_out(bsz, seq, GQA_KV_HEADS, GQA_DIM),
            _query_block_out(bsz, seq, DIFF_HEADS, _query_tile(seq, 2)), blocks(DIFF_HEADS),
            _value_chunk_out(bsz, seq, DIFF_HEADS, 2 * DIFF_DIM)]
    return pl.pallas_call(
        _odd_in_kernel,
        grid=(bsz, seq // tm),
        in_specs=[tok, _mod_spec(layer, 3), _mod_spec(layer, 4), _const_spec(norm_g_row.shape),
                  _const_spec(wz.shape), _const_spec(gq2.shape), _const_spec(gk2.shape), tab, tab],
        out_specs=[o[0] for o in outs],
        out_shape=[o[1] for o in outs],
        compiler_params=_params(2),
        name="odd_in",
    )(x, mod, mod, norm_g_row, wz, gq2, gk2, tabc, tabd)


def _attention_pipeline(n_units, n_chunks, lanes, load_q, stream_refs, finalize, q_scr, s_bufs, p_bufs, acc_ref):
    tk, n = s_bufs[0].shape
    assert n_chunks % 2 == 0 and n_chunks >= 2

    def cat(parts):
        return jnp.concatenate(parts, axis=1) if len(parts) > 1 else parts[0]

    def scores(u, j, s_ref):
        off = pl.multiple_of(j * tk, tk)
        cms = []
        for (k_ref, _), ln in zip(stream_refs(u), lanes):
            s = _dot(k_ref[pl.ds(off, tk), :], q_scr[:, ln])
            s_ref[:, ln] = s
            cms.append(jnp.max(s, axis=0, keepdims=True))
        return cat(cms)

    def values(u, j, p_ref, alpha):
        return cat([alpha[:, ln] * acc_ref[:, ln] + _dot(vt_ref[j], p_ref[:, ln])
                    for (_, vt_ref), ln in zip(stream_refs(u), lanes)])

    def exponentials(cur, m_new):
        for r in range(0, tk, EXP_ROWS):
            p_bufs[cur][r:r + EXP_ROWS, :] = jnp.exp2(s_bufs[cur][r:r + EXP_ROWS, :] - m_new).astype(BF16)

    def step(u, j, carry, cur, other):
        m, cm, alpha_prev = carry
        m_new = jnp.maximum(m, cm)
        alpha = jnp.exp2(m - m_new)
        cm_next = scores(u, j + 1, s_bufs[other])
        acc_ref[...] = values(u, j - 1, p_bufs[other], alpha_prev)
        exponentials(cur, m_new)
        return m_new, cm_next, alpha

    def first_step(u, cm):
        cm1 = scores(u, 1, s_bufs[1])
        acc_ref[...] = jnp.zeros(acc_ref.shape, F32)
        exponentials(0, cm)
        return cm, cm1, jnp.ones((1, n), F32)

    def unit(u, carry):
        def body(jj, carry):
            carry = step(u, 2 * jj + 1, carry, 1, 0)
            return step(u, 2 * jj + 2, carry, 0, 1)

        m, cm_last, alpha_prev = lax.fori_loop(0, (n_chunks - 2) // 2, body, carry)
        m_new = jnp.maximum(m, cm_last)
        alpha = jnp.exp2(m - m_new)
        nxt = jnp.minimum(u + 1, n_units - 1)
        q_scr[...] = load_q(nxt)
        cm_next = scores(nxt, 0, s_bufs[0])
        acc_ref[...] = values(u, n_chunks - 2, p_bufs[0], alpha_prev)
        exponentials(1, m_new)
        finalize(u, values(u, n_chunks - 1, p_bufs[1], alpha))
        return first_step(nxt, cm_next)

    q_scr[...] = load_q(0)
    lax.fori_loop(0, n_units, unit, first_step(0, scores(0, 0, s_bufs[0])))


def _normalised(acc, rows):
    return acc[:rows] * (1.0 / acc[rows:rows + 1])


def _row_lo(shape):
    return lax.broadcasted_iota(jnp.int32, shape, 0) < LANES // 2


def _split_halves_t(qp):
    lo = _row_lo(qp.shape)
    return jnp.concatenate([jnp.where(lo, qp, 0), jnp.where(lo, 0, qp)], axis=1).astype(BF16)


def _attn_call(name, kernel_fn, qt, k, vt, extra, heads_per_step, q_blocks_per_step, out_lanes):
    bsz, n_qblk, n_q, _, tq = qt.shape
    seq = k.shape[2]
    n_groups = n_qblk // q_blocks_per_step
    k_per, v_per = k.shape[1] // n_groups, vt.shape[1] // n_groups
    nb = max(1, min(bsz, ATTN_KEYS_PER_STEP // seq))
    assert bsz % nb == 0
    n_lanes = heads_per_step * tq
    rows, tk = vt.shape[3:]
    return pl.pallas_call(
        functools.partial(kernel_fn, n_units=nb * n_q, n_q=n_q),
        grid=(bsz // nb, n_groups),
        in_specs=[pl.BlockSpec((nb, q_blocks_per_step, n_q, LANES, tq), lambda b, g: (b, g, 0, 0, 0)),
                  pl.BlockSpec((nb, k_per, seq, LANES), lambda b, g: (b, g, 0, 0)),
                  pl.BlockSpec((nb, v_per) + vt.shape[2:], lambda b, g: (b, g, 0, 0, 0))]
        + [_const_spec(e.shape) for e in extra],
        out_specs=pl.BlockSpec((nb, seq, out_lanes), lambda b, g: (b, 0, g)),
        out_shape=jax.ShapeDtypeStruct((bsz, seq, D_MODEL // 2), BF16),
        scratch_shapes=[pltpu.VMEM((LANES, n_lanes), BF16), [pltpu.VMEM((tk, n_lanes), F32)] * 2,
                        [pltpu.VMEM((tk, n_lanes), BF16)] * 2, pltpu.VMEM((rows, n_lanes), F32)],
        compiler_params=_params(2),
        name=name,
    )(qt, k, vt, *extra)


def _mla_attn_kernel(qt_ref, k_ref, vt_ref, o_ref, q_scr, s_bufs, p_bufs, acc_ref, *, n_units, n_q):
    tq = qt_ref.shape[4]

    def finalize(u, acc):
        o = _normalised(acc, MLA_V)
        o_ref[u // n_q, pl.ds(pl.multiple_of((u % n_q) * tq, tq), tq), :] = (
            jnp.concatenate([o[:, :tq], o[:, tq:]], axis=0).T.astype(BF16))

    _attention_pipeline(
        n_units, vt_ref.shape[2], [slice(0, tq), slice(tq, 2 * tq)],
        lambda u: jnp.concatenate([qt_ref[u // n_q, h, u % n_q] for h in range(2)], axis=1),
        lambda u: [(k_ref.at[u // n_q, h], vt_ref.at[u // n_q, h]) for h in range(2)],
        finalize, q_scr, s_bufs, p_bufs, acc_ref)


def _mla_attn_call(qt, k, vt):
    return _attn_call("mla_attn", _mla_attn_kernel, qt, k, vt, [], 2, 2, LANES)


def _gqa_attn_kernel(qt_ref, k_ref, vt_ref, o_ref, q_scr, s_bufs, p_bufs, acc_ref, *, n_units, n_q):
    n_pairs, tq = qt_ref.shape[1], qt_ref.shape[4]

    def finalize(u, acc):
        o = _normalised(acc, GQA_DIM)
        rows = pl.ds(pl.multiple_of((u % n_q) * tq, tq), tq)
        for p in range(n_pairs):
            pair = jnp.concatenate([o[:, (2 * p) * tq:(2 * p + 1) * tq], o[:, (2 * p + 1) * tq:(2 * p + 2) * tq]],
                                   axis=0)
            o_ref[u // n_q, rows, p * LANES:(p + 1) * LANES] = pair.T.astype(BF16)

    _attention_pipeline(
        n_units, vt_ref.shape[2], [slice(0, 2 * n_pairs * tq)],
        lambda u: jnp.concatenate([_split_halves_t(qt_ref[u // n_q, p, u % n_q]) for p in range(n_pairs)], axis=1),
        lambda u: [(k_ref.at[u // n_q, 0], vt_ref.at[u // n_q, 0])],
        finalize, q_scr, s_bufs, p_bufs, acc_ref)


def _gqa_attn_call(qt, k, vt):
    per_kv = qt.shape[1] // GQA_KV_HEADS
    return _attn_call("gqa_attn", _gqa_attn_kernel, qt, k, vt, [], 2 * per_kv, per_kv, per_kv * LANES)


def _diff_attn_kernel(qt_ref, k_ref, vt_ref, lq1_ref, lk1_ref, lq2_ref, lk2_ref, sg_ref, o_ref, q_scr, s_bufs, p_bufs,
                      acc_ref, *, n_units, n_q, lam_init):
    tq = qt_ref.shape[4]

    def finalize(u, acc):
        o = _normalised(acc, 2 * DIFF_DIM)
        lam = (jnp.exp(jnp.sum(lq1_ref[...] * lk1_ref[...], axis=-1, keepdims=True))
               - jnp.exp(jnp.sum(lq2_ref[...] * lk2_ref[...], axis=-1, keepdims=True)) + lam_init)
        od = o[:, :tq] - lam * o[:, tq:]
        od = od * lax.rsqrt(jnp.mean(od * od, axis=0, keepdims=True) + EPS)
        o_ref[u // n_q, pl.ds(pl.multiple_of((u % n_q) * tq, tq), tq), :] = (
            ((od * sg_ref[...]) * (1.0 - lam_init)).T.astype(BF16))

    _attention_pipeline(
        n_units, vt_ref.shape[2], [slice(0, 2 * tq)],
        lambda u: _split_halves_t(qt_ref[u // n_q, 0, u % n_q]),
        lambda u: [(k_ref.at[u // n_q, 0], vt_ref.at[u // n_q, 0])],
        finalize, q_scr, s_bufs, p_bufs, acc_ref)


def _diff_attn_call(qt, k, vt, lq1, lk1, lq2, lk2, subln_g_col, lam_init):
    return _attn_call("diff_attn", functools.partial(_diff_attn_kernel, lam_init=lam_init), qt, k, vt,
                      [lq1, lk1, lq2, lk2, subln_g_col], 2, 1, LANES)


def _pad_cols(w, n):
    return jnp.pad(w, ((0, 0), (0, n)))


def _prep_even(ev_w_in, mla_w_uq, mla_w_ukv):
    o_kpe = MLA_Q_RANK + MLA_KV_RANK
    wz = jnp.concatenate([
        ev_w_in[:, :o_kpe],
        jnp.zeros((D_MODEL, MLA_NOPE), F32), ev_w_in[:, o_kpe:o_kpe + MLA_ROPE],
        jnp.zeros((D_MODEL, LANES - MLA_NOPE - MLA_ROPE), F32),
        ev_w_in[:, o_kpe + MLA_ROPE:]], axis=1)
    wuq = mla_w_uq.reshape(MLA_Q_RANK, MLA_HEADS, MLA_NOPE + MLA_ROPE)
    wuq = jnp.pad(wuq, ((0, 0), (0, 0), (0, LANES - MLA_NOPE - MLA_ROPE))).reshape(MLA_Q_RANK, MLA_HEADS * LANES)
    wukv = mla_w_ukv.reshape(MLA_KV_RANK, MLA_HEADS, MLA_NOPE + MLA_V)
    wuk = jnp.pad(wukv[:, :, :MLA_NOPE], ((0, 0), (0, 0), (0, LANES - MLA_NOPE)))
    wkv = jnp.concatenate([wuk.reshape(MLA_KV_RANK, MLA_HEADS * LANES),
                           wukv[:, :, MLA_NOPE:].reshape(MLA_KV_RANK, MLA_HEADS * MLA_V)], axis=1)
    return wz.astype(BF16), wuq.astype(BF16), wkv.astype(BF16)


def _prep_odd(od_w_in):
    n_qc = GQA_HEADS * GQA_DIM
    n_kv = GQA_KV_HEADS * GQA_DIM

    def dup(w):
        w = w.reshape(D_MODEL, GQA_KV_HEADS, 1, GQA_DIM)
        return jnp.broadcast_to(w, (D_MODEL, GQA_KV_HEADS, 2, GQA_DIM)).reshape(D_MODEL, 2 * n_kv)

    wz = jnp.concatenate([od_w_in[:, :n_qc], dup(od_w_in[:, n_qc:n_qc + n_kv]), od_w_in[:, n_qc + n_kv:]], axis=1)
    return wz.astype(BF16)


def _rope_tables(seq):
    pos = jnp.arange(seq)

    def angles(p, n_freq, theta):
        inv = theta ** (-jnp.arange(0, n_freq, dtype=F32) * 2.0 / (2 * n_freq))
        ang = p.astype(F32)[:, None] * inv[None, :]
        return jnp.cos(ang), jnp.sin(ang)

    one = lambda n: jnp.ones((seq, n), F32)
    zero = lambda n: jnp.zeros((seq, n), F32)

    def stack(c, s1, s2):
        reps = LANES // c.shape[1]
        return jnp.stack([jnp.tile(t, (1, reps)) for t in (c, s1, s2)])

    cos, sin = angles(pos, MLA_ROPE // 2, ROPE_THETA)
    pad = LANES - MLA_NOPE - MLA_ROPE
    mla = stack(jnp.concatenate([one(MLA_NOPE), cos, cos, one(pad)], 1),
                jnp.concatenate([zero(MLA_NOPE), -sin, zero(MLA_ROPE // 2), zero(pad)], 1),
                jnp.concatenate([zero(MLA_NOPE), zero(MLA_ROPE // 2), sin, zero(pad)], 1))
    q = GQA_DIM // 4
    cr, sr = angles(pos // GRID_W, q, AXIAL_THETA)
    cc, sc = angles(pos % GRID_W, q, AXIAL_THETA)
    axial = stack(jnp.concatenate([cr, cr, cc, cc], 1),
                  jnp.concatenate([-sr, zero(q), -sc, zero(q)], 1),
                  jnp.concatenate([zero(q), sr, zero(q), sc], 1))
    hr = DIFF_ROPE // 2
    cd, sd = angles(pos, hr, ROPE_THETA)
    rest = DIFF_DIM - DIFF_ROPE
    diff = stack(jnp.concatenate([cd, cd, one(rest)], 1),
                 jnp.concatenate([-sd, zero(hr), zero(rest)], 1),
                 jnp.concatenate([zero(hr), sd, zero(rest)], 1))
    return mla, axial, diff


def kernel(x_prompt, x_sample, c_prompt, c_sample, w_ada, b_ada, norm_g, ffn_w_in, ffn_w_out, ev_w_in, mla_gq, mla_w_uq, mla_gkv, mla_w_ukv, pool_w, pool_scale, ev_w_out, od_w_in, gqa_gq, gqa_gk, diff_lq1, diff_lk1, diff_lq2, diff_lk2, diff_subln_g, od_w_out, final_g):
    groups = [x_prompt, x_sample]
    n_rows = [x.shape[0] for x in groups]
    c_all = jnp.concatenate([c_prompt, c_sample], axis=0)
    c_all = jnp.pad(c_all, ((0, -c_all.shape[0] % 16), (0, 0)))
    mod = _ada_call(c_all, w_ada, b_ada)
    mods, r0 = [], 0
    for n in n_rows:
        mods.append(mod[:, r0:r0 + n].reshape(DEPTH, n, 1, 9 * D_MODEL))
        r0 += n

    ffn_in = ffn_w_in.astype(BF16)
    ffn_out = ffn_w_out.astype(BF16)
    row = lambda v: v.reshape(1, -1)
    tables = [_rope_tables(x.shape[1]) for x in groups]

    xs = list(groups)
    for i in range(DEPTH):
        j = i // 2
        if i % 2 == 0:
            wz, wuq, wkv = _prep_even(ev_w_in[j], mla_w_uq[j], mla_w_ukv[j])
            w_mix = ev_w_out[j].astype(BF16)
            poolw = pool_w[j].astype(BF16)
        else:
            wz = _prep_odd(od_w_in[j])
            w_mix = od_w_out[j].astype(BF16)
            gq2 = row(jnp.tile(gqa_gq[j], 2))
            gk2 = row(jnp.tile(gqa_gk[j], 2))
            lam_init = 0.8 - 0.6 * math.exp(-0.3 * i)
        for gi in range(len(groups)):
            x, md = xs[gi], mods[gi]
            tab_mla, tab_axial, tab_diff = tables[gi]
            x = _ffn_call(x, md, i, 0, row(norm_g[i, 0]), ffn_in[i, 0], ffn_out[i, 0])
            if i % 2 == 0:
                q, k, v, ob = _even_in_call(x, md, i, row(norm_g[i, 1]), wz, row(mla_gq[j]), wuq, row(mla_gkv[j]),
                                            wkv, tab_mla, poolw, row(pool_scale[j]))
                mix = (_mla_attn_call(q, k, v), ob, w_mix)
            else:
                qc, kc, vc, qd, kd, vd = _odd_in_call(x, md, i, row(norm_g[i, 1]), wz, gq2, gk2, tab_axial, tab_diff)
                oc = _gqa_attn_call(qc, kc, vc)
                od = _diff_attn_call(qd, kd, vd, row(diff_lq1[j]), row(diff_lk1[j]), row(diff_lq2[j]),
                                     row(diff_lk2[j]), diff_subln_g[j].reshape(-1, 1), lam_init)
                mix = (oc, od, w_mix)
            xs[gi] = _ffn_call(x, md, i, 1, row(norm_g[i, 2]), ffn_in[i, 1], ffn_out[i, 1], mix=mix,
                               final_g=row(final_g) if i == DEPTH - 1 else None)
    return tuple(xs)
```

```python
import functools
import math

import jax
import jax.numpy as jnp
from jax import lax
from jax.experimental import pallas as pl
from jax.experimental.pallas import tpu as pltpu

F32 = jnp.float32
BF16 = jnp.bfloat16

D_MODEL = 1024
DEPTH = 4
GRID_W = 64
EPS = 1e-6
D_FF = 2816
ROPE_THETA = 500000.0
AXIAL_THETA = 10000.0
MLA_HEADS = 8
MLA_NOPE = 64
MLA_ROPE = 32
MLA_V = 64
MLA_Q_RANK = 384
MLA_KV_RANK = 256
POOL_WINDOWS = (2, 4, 8, 16)
POOL_GROUP = 128
POOL_WIDTH = POOL_GROUP * len(POOL_WINDOWS)
POOL_HALO = 8
GQA_HEADS = 8
GQA_KV_HEADS = 2
GQA_DIM = 64
DIFF_HEADS = 4
DIFF_DIM = 64
DIFF_ROPE = DIFF_DIM // 4
LANES = 128
LOG2E = math.log2(math.e)

TOKEN_TILE = 512
FFN_TILE = 1024
KV_CHUNK_MIN, KV_CHUNK_MAX = 512, 1024
SCORE_LANES = 2048
ATTN_KEYS_PER_STEP = 8192
EXP_ROWS = 16
DENOM_ROWS = 16
MLA_QSCALE = (MLA_NOPE + MLA_ROPE) ** -0.5 * LOG2E
GQA_QSCALE = GQA_DIM ** -0.5 * LOG2E
DIFF_QSCALE = DIFF_DIM ** -0.5 * LOG2E
ADA_COL_TILE = 1536
FFN_CHUNKS = ((0, 1024), (1024, 1024), (2048, 768))
VMEM_LIMIT = 56 * 1024 * 1024


def _params(n_grid):
    return pltpu.CompilerParams(dimension_semantics=("arbitrary",) * n_grid, vmem_limit_bytes=VMEM_LIMIT)


def _const_spec(shape):
    nd = len(shape)
    return pl.BlockSpec(shape, lambda *_: (0,) * nd, pipeline_mode=pl.Buffered(1))


def _stacked_spec(shape, lead):
    tail = tuple(shape[len(lead):])
    return pl.BlockSpec((None,) * len(lead) + tail, lambda *_: tuple(lead) + (0,) * len(tail),
                        pipeline_mode=pl.Buffered(1))


def _mod_spec(layer, chunk):
    return pl.BlockSpec((None, None, 1, D_MODEL), lambda b, *_: (layer, b, 0, chunk))


def _dot(a, b):
    return jnp.dot(a, b, preferred_element_type=F32)


def _rms(x):
    return x * lax.rsqrt(jnp.mean(x * x, axis=-1, keepdims=True) + EPS)


def _modnorm(x, g, scale, shift):
    return (_rms(x) * g) * (1.0 + scale) + shift


def _silu(x):
    return x * jax.nn.sigmoid(x)


def _rope(x, tab_ref, half):
    width = x.shape[-1]
    fwd = pltpu.roll(x, width - half, axis=1)
    back = pltpu.roll(x, half, axis=1)
    return x * tab_ref[0] + fwd * tab_ref[1] + back * tab_ref[2]


def _store_value_chunks(vt_ref, v, rows):
    tm = v.shape[0]
    vt = jnp.concatenate([v[:, c0:c0 + LANES].T for c0 in range(0, v.shape[1], LANES)], axis=0)
    ones = (lax.broadcasted_iota(jnp.int32, (DENOM_ROWS, tm), 0) == 0).astype(F32)
    for blk in range(v.shape[1] // rows):
        ext = jnp.concatenate([vt[blk * rows:(blk + 1) * rows], ones], axis=0).astype(BF16)
        vt_ref[0, blk, 0] = ext


def _query_tile(seq, n_streams):
    tq = min(SCORE_LANES // n_streams, seq)
    assert tq % TOKEN_TILE == 0 and seq % tq == 0
    return tq


def _query_block_out(bsz, seq, n_blk, tq):
    per_blk = tq // TOKEN_TILE
    spec = pl.BlockSpec((1, n_blk, 1, LANES, TOKEN_TILE), lambda b, t: (b, 0, t // per_blk, 0, t % per_blk))
    return spec, jax.ShapeDtypeStruct((bsz, n_blk, seq // tq, LANES, tq), BF16)


def _kv_chunk(seq):
    tk = max(KV_CHUNK_MIN, min(KV_CHUNK_MAX, seq // 8))
    assert seq % (2 * tk) == 0 and tk % TOKEN_TILE == 0
    return tk


def _value_chunk_out(bsz, seq, n_blk, rows):
    tk = _kv_chunk(seq)
    per_chunk = tk // TOKEN_TILE
    spec = pl.BlockSpec((1, n_blk, 1, rows + DENOM_ROWS, TOKEN_TILE),
                        lambda b, t: (b, 0, t // per_chunk, 0, t % per_chunk))
    return spec, jax.ShapeDtypeStruct((bsz, n_blk, seq // tk, rows + DENOM_ROWS, tk), BF16)


def _ada_kernel(c_ref, w_ref, b_ref, o_ref):
    sc = _silu(c_ref[...]).astype(BF16)
    o_ref[0] = _dot(sc, w_ref[0].astype(BF16)) + b_ref[0]


def _ada_call(c_all, w_ada, b_ada):
    rows = c_all.shape[0]
    n_out = w_ada.shape[-1]
    return pl.pallas_call(
        _ada_kernel,
        grid=(DEPTH, n_out // ADA_COL_TILE),
        in_specs=[
            pl.BlockSpec((rows, D_MODEL), lambda i, n: (0, 0)),
            pl.BlockSpec((1, D_MODEL, ADA_COL_TILE), lambda i, n: (i, 0, n)),
            pl.BlockSpec((1, 1, ADA_COL_TILE), lambda i, n: (i, 0, n)),
        ],
        out_specs=pl.BlockSpec((1, rows, ADA_COL_TILE), lambda i, n: (i, 0, n)),
        out_shape=jax.ShapeDtypeStruct((DEPTH, rows, n_out), F32),
        compiler_params=_params(2),
        name="ada_mod",
    )(c_all, w_ada, b_ada.reshape(DEPTH, 1, n_out))


def _ffn_kernel(*refs, has_mix, has_final):
    it = iter(refs)
    x_ref = next(it)
    if has_mix:
        ma_ref, mb_ref, wmix_ref, g1_ref = next(it), next(it), next(it), next(it)
    sh_ref, sc_ref, gt_ref, ng_ref, win_ref, wout_ref = (next(it) for _ in range(6))
    if has_final:
        fg_ref = next(it)
    o_ref = next(it)

    x = x_ref[0]
    if has_mix:
        half = wmix_ref.shape[0] // 2
        m = _dot(ma_ref[0], wmix_ref[:half, :]) + _dot(mb_ref[0], wmix_ref[half:, :])
        x = x + g1_ref[...] * m
    h = _modnorm(x, ng_ref[...], sc_ref[...], sh_ref[...]).astype(BF16)
    acc = None
    for c0, cn in FFN_CHUNKS:
        g = _dot(h, win_ref[:, c0:c0 + cn])
        u = _dot(h, win_ref[:, D_FF + c0:D_FF + c0 + cn])
        a = (_silu(g) * u).astype(BF16)
        d = _dot(a, wout_ref[c0:c0 + cn, :])
        acc = d if acc is None else acc + d
    y = x + (0.5 * gt_ref[...]) * acc
    if has_final:
        y = _rms(y) * fg_ref[...]
    o_ref[0] = y


def _ffn_call(x, mod, layer, which, norm_g_row, w_in, w_out, mix=None, final_g=None):
    bsz, seq, _ = x.shape
    tm = FFN_TILE
    base = 0 if which == 0 else 6
    tok = pl.BlockSpec((1, tm, D_MODEL), lambda b, t: (b, t, 0))
    args, specs = [x], [tok]
    if mix is not None:
        ma, mb, wmix = mix
        half = pl.BlockSpec((1, tm, D_MODEL // 2), lambda b, t: (b, t, 0))
        args += [ma, mb, wmix, mod]
        specs += [half, half, _const_spec(wmix.shape), _mod_spec(layer, 5)]
    args += [mod, mod, mod, norm_g_row, w_in, w_out]
    specs += [_mod_spec(layer, base), _mod_spec(layer, base + 1), _mod_spec(layer, base + 2),
              _const_spec(norm_g_row.shape), _stacked_spec(w_in.shape, (layer, which)),
              _stacked_spec(w_out.shape, (layer, which))]
    if final_g is not None:
        args.append(final_g)
        specs.append(_const_spec(final_g.shape))
    return pl.pallas_call(
        functools.partial(_ffn_kernel, has_mix=mix is not None, has_final=final_g is not None),
        grid=(bsz, seq // tm),
        in_specs=specs,
        out_specs=tok,
        out_shape=jax.ShapeDtypeStruct(x.shape, F32),
        compiler_params=_params(2),
        name="ffn",
    )(*args)


def _pool(pz, halo, t_idx, n_tiles, seq, poolw_ref, pscale_ref):
    tm = pz.shape[0]
    prev = jnp.where(t_idx > 0, halo[:POOL_HALO], 0.0)
    nxt = jnp.where(t_idx < n_tiles - 1, halo[POOL_HALO:], 0.0)
    ext = jnp.concatenate([prev, pz, nxt], axis=0)
    n_ext = ext.shape[0]
    pos = t_idx * tm + lax.broadcasted_iota(jnp.int32, (tm, 1), 0)
    outs = []
    for g, w in enumerate(POOL_WINDOWS):
        sl = slice(g * POOL_GROUP, (g + 1) * POOL_GROUP)
        run = ext[:, sl]
        span = 1
        while span < w:
            run = run + pltpu.roll(run, n_ext - span, axis=0)
            span *= 2
        start = POOL_HALO - w // 2
        if start:
            run = pltpu.roll(run, n_ext - start, axis=0)
        win = run[:tm]
        cnt = jnp.minimum(pos + (w - w // 2), seq) - jnp.maximum(pos - w // 2, 0)
        d = (win / cnt.astype(F32) - pz[:, sl]).astype(BF16)
        outs.append(_dot(d, poolw_ref[g]))
    return jnp.concatenate(outs, axis=-1) * pscale_ref[...]


def _even_in_kernel(x_ref, xp_ref, xn_ref, sh_ref, sc_ref, ng_ref, wz_ref, gq_ref, wuq_ref, gkv_ref, wkv_ref,
                    tab_ref, poolw_ref, pscale_ref, q_ref, k_ref, v_ref, ob_ref, *, seq):
    t_idx = pl.program_id(1)
    n_tiles = pl.num_programs(1)
    g, sc, sh = ng_ref[...], sc_ref[...], sh_ref[...]
    h = _modnorm(x_ref[0], g, sc, sh).astype(BF16)
    z = _dot(h, wz_ref[...])
    o_kpe = MLA_Q_RANK + MLA_KV_RANK
    o_pz = o_kpe + LANES

    xh = jnp.concatenate([xp_ref[0], xn_ref[0]], axis=0)
    hh = _modnorm(xh, g, sc, sh).astype(BF16)
    halo = _dot(hh, wz_ref[:, o_pz:])
    ob_ref[0] = _pool(z[:, o_pz:], halo, t_idx, n_tiles, seq, poolw_ref, pscale_ref).astype(BF16)

    cq = (_rms(z[:, :MLA_Q_RANK]) * gq_ref[...]).astype(BF16)
    ckv = (_rms(z[:, MLA_Q_RANK:o_kpe]) * gkv_ref[...]).astype(BF16)
    q_all = _dot(cq, wuq_ref[...])
    kv_all = _dot(ckv, wkv_ref[...])
    kpe = _rope(z[:, o_kpe:o_pz], tab_ref, MLA_ROPE // 2)
    for hd in range(MLA_HEADS):
        sl = slice(hd * LANES, (hd + 1) * LANES)
        q_ref[0, hd, 0] = (_rope(q_all[:, sl], tab_ref, MLA_ROPE // 2) * MLA_QSCALE).T.astype(BF16)
        k_ref[0, hd] = (kv_all[:, sl] + kpe).astype(BF16)
    _store_value_chunks(v_ref, kv_all[:, MLA_HEADS * LANES:], MLA_V)


def _even_in_call(x, mod, layer, norm_g_row, wz, gq, wuq, gkv, wkv, tab, poolw, pscale):
    bsz, seq, _ = x.shape
    tm = TOKEN_TILE
    hb = tm // POOL_HALO
    n_hblk = seq // POOL_HALO
    tok = pl.BlockSpec((1, tm, D_MODEL), lambda b, t: (b, t, 0))
    prev = pl.BlockSpec((1, POOL_HALO, D_MODEL), lambda b, t: (b, jnp.maximum(t * hb - 1, 0), 0))
    nxt = pl.BlockSpec((1, POOL_HALO, D_MODEL), lambda b, t: (b, jnp.minimum((t + 1) * hb, n_hblk - 1), 0))
    heads = pl.BlockSpec((1, MLA_HEADS, tm, LANES), lambda b, t: (b, 0, t, 0))
    heads_t, out_heads_t = _query_block_out(bsz, seq, MLA_HEADS, _query_tile(seq, 2))
    half = pl.BlockSpec((1, tm, D_MODEL // 2), lambda b, t: (b, t, 0))
    out_heads = jax.ShapeDtypeStruct((bsz, MLA_HEADS, seq, LANES), BF16)
    out_half = jax.ShapeDtypeStruct((bsz, seq, D_MODEL // 2), BF16)
    vt_spec, vt_shape = _value_chunk_out(bsz, seq, MLA_HEADS, MLA_V)
    return pl.pallas_call(
        functools.partial(_even_in_kernel, seq=seq),
        grid=(bsz, seq // tm),
        in_specs=[tok, prev, nxt, _mod_spec(layer, 3), _mod_spec(layer, 4), _const_spec(norm_g_row.shape),
                  _const_spec(wz.shape), _const_spec(gq.shape), _const_spec(wuq.shape), _const_spec(gkv.shape),
                  _const_spec(wkv.shape), pl.BlockSpec((3, tm, LANES), lambda b, t: (0, t, 0)),
                  _const_spec(poolw.shape), _const_spec(pscale.shape)],
        out_specs=[heads_t, heads, vt_spec, half],
        out_shape=[out_heads_t, out_heads, vt_shape, out_half],
        compiler_params=_params(2),
        name="even_in",
    )(x, x, x, mod, mod, norm_g_row, wz, gq, wuq, gkv, wkv, tab, poolw, pscale)


def _pair_norm(x, gain):
    lo_mask = lax.broadcasted_iota(jnp.int32, x.shape, 1) < GQA_DIM
    sq = x * x
    lo = jnp.sum(jnp.where(lo_mask, sq, 0.0), axis=-1, keepdims=True)
    hi = jnp.sum(jnp.where(lo_mask, 0.0, sq), axis=-1, keepdims=True)
    inv = jnp.where(lo_mask, lax.rsqrt(lo / GQA_DIM + EPS), lax.rsqrt(hi / GQA_DIM + EPS))
    return x * inv * gain


def _odd_in_kernel(x_ref, sh_ref, sc_ref, ng_ref, wz_ref, gq_ref, gk_ref, tabc_ref, tabd_ref,
                   qc_ref, kc_ref, vc_ref, qd_ref, kd_ref, vd_ref):
    h = _modnorm(x_ref[0], ng_ref[...], sc_ref[...], sh_ref[...]).astype(BF16)
    z = _dot(h, wz_ref[...])
    n_qc = GQA_HEADS * GQA_DIM // LANES
    n_kv = GQA_KV_HEADS
    n_d = DIFF_HEADS
    col = 0
    for p in range(n_qc):
        blk = _rope(_pair_norm(z[:, col:col + LANES], gq_ref[...]), tabc_ref, GQA_DIM // 4)
        qc_ref[0, p, 0] = (blk * GQA_QSCALE).T.astype(BF16)
        col += LANES
    for j in range(n_kv):
        blk = z[:, col:col + LANES]
        kc_ref[0, j] = _rope(_pair_norm(blk, gk_ref[...]), tabc_ref, GQA_DIM // 4).astype(BF16)
        col += LANES
    _store_value_chunks(vc_ref, z[:, col:col + n_kv * GQA_DIM], GQA_DIM)
    col += n_kv * GQA_DIM
    for hd in range(n_d):
        qd_ref[0, hd, 0] = (_rope(z[:, col:col + LANES], tabd_ref, DIFF_ROPE // 2) * DIFF_QSCALE).T.astype(BF16)
        col += LANES
    for hd in range(n_d):
        kd_ref[0, hd] = _rope(z[:, col:col + LANES], tabd_ref, DIFF_ROPE // 2).astype(BF16)
        col += LANES
    _store_value_chunks(vd_ref, z[:, col:], 2 * DIFF_DIM)


def _odd_in_call(x, mod, layer, norm_g_row, wz, gq2, gk2, tabc, tabd):
    bsz, seq, _ = x.shape
    tm = TOKEN_TILE
    tok = pl.BlockSpec((1, tm, D_MODEL), lambda b, t: (b, t, 0))
    tab = pl.BlockSpec((3, tm, LANES), lambda b, t: (0, t, 0))

    def blocks(n):
        return (pl.BlockSpec((1, n, tm, LANES), lambda b, t: (b, 0, t, 0)),
                jax.ShapeDtypeStruct((bsz, n, seq, LANES), BF16))

    n_qc = GQA_HEADS * GQA_DIM // LANES
    outs = [_query_block_out(bsz, seq, n_qc, _query_tile(seq, 2 * n_qc // GQA_KV_HEADS)), blocks(GQA_KV_HEADS),
            _value_chunk_out(bsz, seq, GQA_KV_HEADS, GQA_DIM),
            _query_block_out(bsz, seq, DIFF_HEADS, _query_tile(seq, 2)), blocks(DIFF_HEADS),
            _value_chunk_out(bsz, seq, DIFF_HEADS, 2 * DIFF_DIM)]
    return pl.pallas_call(
        _odd_in_kernel,
        grid=(bsz, seq // tm),
        in_specs=[tok, _mod_spec(layer, 3), _mod_spec(layer, 4), _const_spec(norm_g_row.shape),
                  _const_spec(wz.shape), _const_spec(gq2.shape), _const_spec(gk2.shape), tab, tab],
        out_specs=[o[0] for o in outs],
        out_shape=[o[1] for o in outs],
        compiler_params=_params(2),
        name="odd_in",
    )(x, mod, mod, norm_g_row, wz, gq2, gk2, tabc, tabd)


def _attention_pipeline(n_units, n_chunks, lanes, load_q, stream_refs, finalize, q_scr, s_bufs, p_bufs, acc_ref):
    tk, n = s_bufs[0].shape
    assert n_chunks % 2 == 0 and n_chunks >= 2

    def cat(parts):
        return jnp.concatenate(parts, axis=1) if len(parts) > 1 else parts[0]

    def scores(u, j, s_ref):
        off = pl.multiple_of(j * tk, tk)
        cms = []
        for (k_ref, _), ln in zip(stream_refs(u), lanes):
            s = _dot(k_ref[pl.ds(off, tk), :], q_scr[:, ln])
            s_ref[:, ln] = s
            cms.append(jnp.max(s, axis=0, keepdims=True))
        return cat(cms)

    def values(u, j, p_ref, alpha):
        return cat([alpha[:, ln] * acc_ref[:, ln] + _dot(vt_ref[j], p_ref[:, ln])
                    for (_, vt_ref), ln in zip(stream_refs(u), lanes)])

    def exponentials(cur, m_new):
        for r in range(0, tk, EXP_ROWS):
            p_bufs[cur][r:r + EXP_ROWS, :] = jnp.exp2(s_bufs[cur][r:r + EXP_ROWS, :] - m_new).astype(BF16)

    def step(u, j, carry, cur, other):
        m, cm, alpha_prev = carry
        m_new = jnp.maximum(m, cm)
        alpha = jnp.exp2(m - m_new)
        cm_next = scores(u, j + 1, s_bufs[other])
        acc_ref[...] = values(u, j - 1, p_bufs[other], alpha_prev)
        exponentials(cur, m_new)
        return m_new, cm_next, alpha

    def first_step(u, cm):
        cm1 = scores(u, 1, s_bufs[1])
        acc_ref[...] = jnp.zeros(acc_ref.shape, F32)
        exponentials(0, cm)
        return cm, cm1, jnp.ones((1, n), F32)

    def unit(u, carry):
        def body(jj, carry):
            carry = step(u, 2 * jj + 1, carry, 1, 0)
            return step(u, 2 * jj + 2, carry, 0, 1)

        m, cm_last, alpha_prev = lax.fori_loop(0, (n_chunks - 2) // 2, body, carry)
        m_new = jnp.maximum(m, cm_last)
        alpha = jnp.exp2(m - m_new)
        nxt = jnp.minimum(u + 1, n_units - 1)
        q_scr[...] = load_q(nxt)
        cm_next = scores(nxt, 0, s_bufs[0])
        acc_ref[...] = values(u, n_chunks - 2, p_bufs[0], alpha_prev)
        exponentials(1, m_new)
        finalize(u, values(u, n_chunks - 1, p_bufs[1], alpha))
        return first_step(nxt, cm_next)

    q_scr[...] = load_q(0)
    lax.fori_loop(0, n_units, unit, first_step(0, scores(0, 0, s_bufs[0])))


def _normalised(acc, rows):
    return acc[:rows] * (1.0 / acc[rows:rows + 1])


def _row_lo(shape):
    return lax.broadcasted_iota(jnp.int32, shape, 0) < LANES // 2


def _split_halves_t(qp):
    lo = _row_lo(qp.shape)
    return jnp.concatenate([jnp.where(lo, qp, 0), jnp.where(lo, 0, qp)], axis=1).astype(BF16)


def _attn_call(name, kernel_fn, qt, k, vt, extra, heads_per_step, q_blocks_per_step, out_lanes):
    bsz, n_qblk, n_q, _, tq = qt.shape
    seq = k.shape[2]
    n_groups = n_qblk // q_blocks_per_step
    k_per, v_per = k.shape[1] // n_groups, vt.shape[1] // n_groups
    nb = max(1, min(bsz, ATTN_KEYS_PER_STEP // seq))
    assert bsz % nb == 0
    n_lanes = heads_per_step * tq
    rows, tk = vt.shape[3:]
    return pl.pallas_call(
        functools.partial(kernel_fn, n_units=nb * n_q, n_q=n_q),
        grid=(bsz // nb, n_groups),
        in_specs=[pl.BlockSpec((nb, q_blocks_per_step, n_q, LANES, tq), lambda b, g: (b, g, 0, 0, 0)),
                  pl.BlockSpec((nb, k_per, seq, LANES), lambda b, g: (b, g, 0, 0)),
                  pl.BlockSpec((nb, v_per) + vt.shape[2:], lambda b, g: (b, g, 0, 0, 0))]
        + [_const_spec(e.shape) for e in extra],
        out_specs=pl.BlockSpec((nb, seq, out_lanes), lambda b, g: (b, 0, g)),
        out_shape=jax.ShapeDtypeStruct((bsz, seq, D_MODEL // 2), BF16),
        scratch_shapes=[pltpu.VMEM((LANES, n_lanes), BF16), [pltpu.VMEM((tk, n_lanes), F32)] * 2,
                        [pltpu.VMEM((tk, n_lanes), BF16)] * 2, pltpu.VMEM((rows, n_lanes), F32)],
        compiler_params=_params(2),
        name=name,
    )(qt, k, vt, *extra)


def _mla_attn_kernel(qt_ref, k_ref, vt_ref, o_ref, q_scr, s_bufs, p_bufs, acc_ref, *, n_units, n_q):
    tq = qt_ref.shape[4]

    def finalize(u, acc):
        o = _normalised(acc, MLA_V)
        o_ref[u // n_q, pl.ds(pl.multiple_of((u % n_q) * tq, tq), tq), :] = (
            jnp.concatenate([o[:, :tq], o[:, tq:]], axis=0).T.astype(BF16))

    _attention_pipeline(
        n_units, vt_ref.shape[2], [slice(0, tq), slice(tq, 2 * tq)],
        lambda u: jnp.concatenate([qt_ref[u // n_q, h, u % n_q] for h in range(2)], axis=1),
        lambda u: [(k_ref.at[u // n_q, h], vt_ref.at[u // n_q, h]) for h in range(2)],
        finalize, q_scr, s_bufs, p_bufs, acc_ref)


def _mla_attn_call(qt, k, vt):
    return _attn_call("mla_attn", _mla_attn_kernel, qt, k, vt, [], 2, 2, LANES)


def _gqa_attn_kernel(qt_ref, k_ref, vt_ref, o_ref, q_scr, s_bufs, p_bufs, acc_ref, *, n_units, n_q):
    n_pairs, tq = qt_ref.shape[1], qt_ref.shape[4]

    def finalize(u, acc):
        o = _normalised(acc, GQA_DIM)
        rows = pl.ds(pl.multiple_of((u % n_q) * tq, tq), tq)
        for p in range(n_pairs):
            pair = jnp.concatenate([o[:, (2 * p) * tq:(2 * p + 1) * tq], o[:, (2 * p + 1) * tq:(2 * p + 2) * tq]],
                                   axis=0)
            o_ref[u // n_q, rows, p * LANES:(p + 1) * LANES] = pair.T.astype(BF16)

    _attention_pipeline(
        n_units, vt_ref.shape[2], [slice(0, 2 * n_pairs * tq)],
        lambda u: jnp.concatenate([_split_halves_t(qt_ref[u // n_q, p, u % n_q]) for p in range(n_pairs)], axis=1),
        lambda u: [(k_ref.at[u // n_q, 0], vt_ref.at[u // n_q, 0])],
        finalize, q_scr, s_bufs, p_bufs, acc_ref)


def _gqa_attn_call(qt, k, vt):
    per_kv = qt.shape[1] // GQA_KV_HEADS
    return _attn_call("gqa_attn", _gqa_attn_kernel, qt, k, vt, [], 2 * per_kv, per_kv, per_kv * LANES)


def _diff_attn_kernel(qt_ref, k_ref, vt_ref, lq1_ref, lk1_ref, lq2_ref, lk2_ref, sg_ref, o_ref, q_scr, s_bufs, p_bufs,
                      acc_ref, *, n_units, n_q, lam_init):
    tq = qt_ref.shape[4]

    def finalize(u, acc):
        o = _normalised(acc, 2 * DIFF_DIM)
        lam = (jnp.exp(jnp.sum(lq1_ref[...] * lk1_ref[...], axis=-1, keepdims=True))
               - jnp.exp(jnp.sum(lq2_ref[...] * lk2_ref[...], axis=-1, keepdims=True)) + lam_init)
        od = o[:, :tq] - lam * o[:, tq:]
        od = od * lax.rsqrt(jnp.mean(od * od, axis=0, keepdims=True) + EPS)
        o_ref[u // n_q, pl.ds(pl.multiple_of((u % n_q) * tq, tq), tq), :] = (
            ((od * sg_ref[...]) * (1.0 - lam_init)).T.astype(BF16))

    _attention_pipeline(
        n_units, vt_ref.shape[2], [slice(0, 2 * tq)],
        lambda u: _split_halves_t(qt_ref[u // n_q, 0, u % n_q]),
        lambda u: [(k_ref.at[u // n_q, 0], vt_ref.at[u // n_q, 0])],
        finalize, q_scr, s_bufs, p_bufs, acc_ref)


def _diff_attn_call(qt, k, vt, lq1, lk1, lq2, lk2, subln_g_col, lam_init):
    return _attn_call("diff_attn", functools.partial(_diff_attn_kernel, lam_init=lam_init), qt, k, vt,
                      [lq1, lk1, lq2, lk2, subln_g_col], 2, 1, LANES)


def _prep_even(ev_w_in, mla_w_uq, mla_w_ukv):
    o_kpe = MLA_Q_RANK + MLA_KV_RANK
    wz = jnp.concatenate([
        ev_w_in[:, :o_kpe],
        jnp.zeros((D_MODEL, MLA_NOPE), F32), ev_w_in[:, o_kpe:o_kpe + MLA_ROPE],
        jnp.zeros((D_MODEL, LANES - MLA_NOPE - MLA_ROPE), F32),
        ev_w_in[:, o_kpe + MLA_ROPE:]], axis=1)
    wuq = mla_w_uq.reshape(MLA_Q_RANK, MLA_HEADS, MLA_NOPE + MLA_ROPE)
    wuq = jnp.pad(wuq, ((0, 0), (0, 0), (0, LANES - MLA_NOPE - MLA_ROPE))).reshape(MLA_Q_RANK, MLA_HEADS * LANES)
    wukv = mla_w_ukv.reshape(MLA_KV_RANK, MLA_HEADS, MLA_NOPE + MLA_V)
    wuk = jnp.pad(wukv[:, :, :MLA_NOPE], ((0, 0), (0, 0), (0, LANES - MLA_NOPE)))
    wkv = jnp.concatenate([wuk.reshape(MLA_KV_RANK, MLA_HEADS * LANES),
                           wukv[:, :, MLA_NOPE:].reshape(MLA_KV_RANK, MLA_HEADS * MLA_V)], axis=1)
    return wz.astype(BF16), wuq.astype(BF16), wkv.astype(BF16)


def _prep_odd(od_w_in):
    n_qc = GQA_HEADS * GQA_DIM
    n_kv = GQA_KV_HEADS * GQA_DIM

    def dup(w):
        w = w.reshape(D_MODEL, GQA_KV_HEADS, 1, GQA_DIM)
        return jnp.broadcast_to(w, (D_MODEL, GQA_KV_HEADS, 2, GQA_DIM)).reshape(D_MODEL, 2 * n_kv)

    wz = jnp.concatenate([od_w_in[:, :n_qc], dup(od_w_in[:, n_qc:n_qc + n_kv]), od_w_in[:, n_qc + n_kv:]], axis=1)
    return wz.astype(BF16)


def _rope_tables(seq):
    pos = jnp.arange(seq)

    def angles(p, n_freq, theta):
        inv = theta ** (-jnp.arange(0, n_freq, dtype=F32) * 2.0 / (2 * n_freq))
        ang = p.astype(F32)[:, None] * inv[None, :]
        return jnp.cos(ang), jnp.sin(ang)

    one = lambda n: jnp.ones((seq, n), F32)
    zero = lambda n: jnp.zeros((seq, n), F32)

    def stack(c, s1, s2):
        reps = LANES // c.shape[1]
        return jnp.stack([jnp.tile(t, (1, reps)) for t in (c, s1, s2)])

    cos, sin = angles(pos, MLA_ROPE // 2, ROPE_THETA)
    pad = LANES - MLA_NOPE - MLA_ROPE
    mla = stack(jnp.concatenate([one(MLA_NOPE), cos, cos, one(pad)], 1),
                jnp.concatenate([zero(MLA_NOPE), -sin, zero(MLA_ROPE // 2), zero(pad)], 1),
                jnp.concatenate([zero(MLA_NOPE), zero(MLA_ROPE // 2), sin, zero(pad)], 1))
    q = GQA_DIM // 4
    cr, sr = angles(pos // GRID_W, q, AXIAL_THETA)
    cc, sc = angles(pos % GRID_W, q, AXIAL_THETA)
    axial = stack(jnp.concatenate([cr, cr, cc, cc], 1),
                  jnp.concatenate([-sr, zero(q), -sc, zero(q)], 1),
                  jnp.concatenate([zero(q), sr, zero(q), sc], 1))
    hr = DIFF_ROPE // 2
    cd, sd = angles(pos, hr, ROPE_THETA)
    rest = DIFF_DIM - DIFF_ROPE
    diff = stack(jnp.concatenate([cd, cd, one(rest)], 1),
                 jnp.concatenate([-sd, zero(hr), zero(rest)], 1),
                 jnp.concatenate([zero(hr), sd, zero(rest)], 1))
    return mla, axial, diff


def kernel(x_prompt, x_sample, c_prompt, c_sample, w_ada, b_ada, norm_g, ffn_w_in, ffn_w_out, ev_w_in, mla_gq, mla_w_uq, mla_gkv, mla_w_ukv, pool_w, pool_scale, ev_w_out, od_w_in, gqa_gq, gqa_gk, diff_lq1, diff_lk1, diff_lq2, diff_lk2, diff_subln_g, od_w_out, final_g):
    groups = [x_prompt, x_sample]
    n_rows = [x.shape[0] for x in groups]
    c_all = jnp.concatenate([c_prompt, c_sample], axis=0)
    c_all = jnp.pad(c_all, ((0, -c_all.shape[0] % 16), (0, 0)))
    mod = _ada_call(c_all, w_ada, b_ada)
    mods, r0 = [], 0
    for n in n_rows:
        mods.append(mod[:, r0:r0 + n].reshape(DEPTH, n, 1, 9 * D_MODEL))
        r0 += n

    ffn_in = ffn_w_in.astype(BF16)
    ffn_out = ffn_w_out.astype(BF16)
    row = lambda v: v.reshape(1, -1)
    tables = [_rope_tables(x.shape[1]) for x in groups]

    xs = list(groups)
    for i in range(DEPTH):
        j = i // 2
        if i % 2 == 0:
            wz, wuq, wkv = _prep_even(ev_w_in[j], mla_w_uq[j], mla_w_ukv[j])
            w_mix = ev_w_out[j].astype(BF16)
            poolw = pool_w[j].astype(BF16)
        else:
            wz = _prep_odd(od_w_in[j])
            w_mix = od_w_out[j].astype(BF16)
            gq2 = row(jnp.tile(gqa_gq[j], 2))
            gk2 = row(jnp.tile(gqa_gk[j], 2))
            lam_init = 0.8 - 0.6 * math.exp(-0.3 * i)
        for gi in range(len(groups)):
            x, md = xs[gi], mods[gi]
            tab_mla, tab_axial, tab_diff = tables[gi]
            x = _ffn_call(x, md, i, 0, row(norm_g[i, 0]), ffn_in, ffn_out)
            if i % 2 == 0:
                q, k, v, ob = _even_in_call(x, md, i, row(norm_g[i, 1]), wz, row(mla_gq[j]), wuq, row(mla_gkv[j]),
                                            wkv, tab_mla, poolw, row(pool_scale[j]))
                mix = (_mla_attn_call(q, k, v), ob, w_mix)
            else:
                qc, kc, vc, qd, kd, vd = _odd_in_call(x, md, i, row(norm_g[i, 1]), wz, gq2, gk2, tab_axial, tab_diff)
                oc = _gqa_attn_call(qc, kc, vc)
                od = _diff_attn_call(qd, kd, vd, row(diff_lq1[j]), row(diff_lk1[j]), row(diff_lq2[j]),
                                     row(diff_lk2[j]), diff_subln_g[j].reshape(-1, 1), lam_init)
                mix = (oc, od, w_mix)
            xs[gi] = _ffn_call(x, md, i, 1, row(norm_g[i, 2]), ffn_in, ffn_out, mix=mix,
                               final_g=row(final_g) if i == DEPTH - 1 else None)
    return tuple(xs)
```

```python
import functools
import math

import jax
import jax.numpy as jnp
from jax import lax
from jax.experimental import pallas as pl
from jax.experimental.pallas import tpu as pltpu

F32 = jnp.float32
BF16 = jnp.bfloat16

D_MODEL = 1024
DEPTH = 4
GRID_W = 64
EPS = 1e-6
D_FF = 2816
ROPE_THETA = 500000.0
AXIAL_THETA = 10000.0
MLA_HEADS = 8
MLA_NOPE = 64
MLA_ROPE = 32
MLA_V = 64
MLA_Q_RANK = 384
MLA_KV_RANK = 256
POOL_WINDOWS = (2, 4, 8, 16)
POOL_GROUP = 128
POOL_WIDTH = POOL_GROUP * len(POOL_WINDOWS)
POOL_HALO = 8
GQA_HEADS = 8
GQA_KV_HEADS = 2
GQA_DIM = 64
DIFF_HEADS = 4
DIFF_DIM = 64
DIFF_ROPE = DIFF_DIM // 4
LANES = 128
LOG2E = math.log2(math.e)

TOKEN_TILE = 512
FFN_TILE = 1024
KV_CHUNK_MIN, KV_CHUNK_MAX = 512, 1024
SCORE_LANES = 2048
ATTN_KEYS_PER_STEP = 8192
EXP_ROWS = 16
DENOM_ROWS = 16
MLA_QSCALE = (MLA_NOPE + MLA_ROPE) ** -0.5 * LOG2E
GQA_QSCALE = GQA_DIM ** -0.5 * LOG2E
DIFF_QSCALE = DIFF_DIM ** -0.5 * LOG2E
ADA_COL_TILE = 1536
FFN_CHUNKS = ((0, 1024), (1024, 1024), (2048, 768))
VMEM_LIMIT = 56 * 1024 * 1024


def _params(n_grid):
    return pltpu.CompilerParams(dimension_semantics=("arbitrary",) * n_grid, vmem_limit_bytes=VMEM_LIMIT)


def _const_spec(shape):
    nd = len(shape)
    return pl.BlockSpec(shape, lambda *_: (0,) * nd, pipeline_mode=pl.Buffered(1))


def _stacked_spec(shape, lead):
    tail = tuple(shape[len(lead):])
    return pl.BlockSpec((None,) * len(lead) + tail, lambda *_: tuple(lead) + (0,) * len(tail),
                        pipeline_mode=pl.Buffered(1))


def _mod_spec(layer, chunk):
    return pl.BlockSpec((None, None, 1, D_MODEL), lambda b, *_: (layer, b, 0, chunk))


def _dot(a, b):
    return jnp.dot(a, b, preferred_element_type=F32)


def _rms(x):
    return x * lax.rsqrt(jnp.mean(x * x, axis=-1, keepdims=True) + EPS)


def _modnorm(x, g, scale, shift):
    return (_rms(x) * g) * (1.0 + scale) + shift


def _silu(x):
    return x * jax.nn.sigmoid(x)


def _rope(x, tab_ref, half):
    width = x.shape[-1]
    fwd = pltpu.roll(x, width - half, axis=1)
    back = pltpu.roll(x, half, axis=1)
    return x * tab_ref[0] + fwd * tab_ref[1] + back * tab_ref[2]


def _store_value_chunks(vt_ref, v, rows):
    tm = v.shape[0]
    vt = jnp.concatenate([v[:, c0:c0 + LANES].T for c0 in range(0, v.shape[1], LANES)], axis=0)
    ones = (lax.broadcasted_iota(jnp.int32, (DENOM_ROWS, tm), 0) == 0).astype(F32)
    for blk in range(v.shape[1] // rows):
        ext = jnp.concatenate([vt[blk * rows:(blk + 1) * rows], ones], axis=0).astype(BF16)
        vt_ref[0, blk, 0] = ext


def _query_tile(seq, n_streams):
    tq = min(SCORE_LANES // n_streams, seq)
    assert tq % TOKEN_TILE == 0 and seq % tq == 0
    return tq


def _query_block_out(bsz, seq, n_blk, tq):
    per_blk = tq // TOKEN_TILE
    spec = pl.BlockSpec((1, n_blk, 1, LANES, TOKEN_TILE), lambda b, t: (b, 0, t // per_blk, 0, t % per_blk))
    return spec, jax.ShapeDtypeStruct((bsz, n_blk, seq // tq, LANES, tq), BF16)


def _kv_chunk(seq):
    tk = max(KV_CHUNK_MIN, min(KV_CHUNK_MAX, seq // 8))
    assert seq % (2 * tk) == 0 and tk % TOKEN_TILE == 0
    return tk


def _value_chunk_out(bsz, seq, n_blk, rows):
    tk = _kv_chunk(seq)
    per_chunk = tk // TOKEN_TILE
    spec = pl.BlockSpec((1, n_blk, 1, rows + DENOM_ROWS, TOKEN_TILE),
                        lambda b, t: (b, 0, t // per_chunk, 0, t % per_chunk))
    return spec, jax.ShapeDtypeStruct((bsz, n_blk, seq // tk, rows + DENOM_ROWS, tk), BF16)


def _ada_kernel(c_ref, w_ref, b_ref, o_ref):
    sc = _silu(c_ref[...]).astype(BF16)
    o_ref[0] = _dot(sc, w_ref[0].astype(BF16)) + b_ref[0]


def _ada_call(c_all, w_ada, b_ada):
    rows = c_all.shape[0]
    n_out = w_ada.shape[-1]
    return pl.pallas_call(
        _ada_kernel,
        grid=(DEPTH, n_out // ADA_COL_TILE),
        in_specs=[
            pl.BlockSpec((rows, D_MODEL), lambda i, n: (0, 0)),
            pl.BlockSpec((1, D_MODEL, ADA_COL_TILE), lambda i, n: (i, 0, n)),
            pl.BlockSpec((1, 1, ADA_COL_TILE), lambda i, n: (i, 0, n)),
        ],
        out_specs=pl.BlockSpec((1, rows, ADA_COL_TILE), lambda i, n: (i, 0, n)),
        out_shape=jax.ShapeDtypeStruct((DEPTH, rows, n_out), F32),
        compiler_params=_params(2),
        name="ada_mod",
    )(c_all, w_ada, b_ada.reshape(DEPTH, 1, n_out))


def _ffn_kernel(*refs, has_mix, has_final):
    it = iter(refs)
    x_ref = next(it)
    if has_mix:
        ma_ref, mb_ref, wmix_ref, g1_ref = next(it), next(it), next(it), next(it)
    sh_ref, sc_ref, gt_ref, ng_ref, win_ref, wout_ref = (next(it) for _ in range(6))
    if has_final:
        fg_ref = next(it)
    o_ref = next(it)

    x = x_ref[0]
    if has_mix:
        half = wmix_ref.shape[0] // 2
        m = _dot(ma_ref[0], wmix_ref[:half, :]) + _dot(mb_ref[0], wmix_ref[half:, :])
        x = x + g1_ref[...] * m
    h = _modnorm(x, ng_ref[...], sc_ref[...], sh_ref[...]).astype(BF16)
    acc = None
    for c0, cn in FFN_CHUNKS:
        g = _dot(h, win_ref[:, c0:c0 + cn])
        u = _dot(h, win_ref[:, D_FF + c0:D_FF + c0 + cn])
        a = (_silu(g) * u).astype(BF16)
        d = _dot(a, wout_ref[c0:c0 + cn, :])
        acc = d if acc is None else acc + d
    y = x + (0.5 * gt_ref[...]) * acc
    if has_final:
        y = _rms(y) * fg_ref[...]
    o_ref[0] = y


def _ffn_call(x, mod, layer, which, norm_g_row, w_in, w_out, mix=None, final_g=None):
    bsz, seq, _ = x.shape
    tm = FFN_TILE
    base = 0 if which == 0 else 6
    tok = pl.BlockSpec((1, tm, D_MODEL), lambda b, t: (b, t, 0))
    args, specs = [x], [tok]
    if mix is not None:
        ma, mb, wmix = mix
        half = pl.BlockSpec((1, tm, D_MODEL // 2), lambda b, t: (b, t, 0))
        args += [ma, mb, wmix, mod]
        specs += [half, half, _const_spec(wmix.shape), _mod_spec(layer, 5)]
    args += [mod, mod, mod, norm_g_row, w_in, w_out]
    specs += [_mod_spec(layer, base), _mod_spec(layer, base + 1), _mod_spec(layer, base + 2),
              _const_spec(norm_g_row.shape), _stacked_spec(w_in.shape, (layer, which)),
              _stacked_spec(w_out.shape, (layer, which))]
    if final_g is not None:
        args.append(final_g)
        specs.append(_const_spec(final_g.shape))
    return pl.pallas_call(
        functools.partial(_ffn_kernel, has_mix=mix is not None, has_final=final_g is not None),
        grid=(bsz, seq // tm),
        in_specs=specs,
        out_specs=tok,
        out_shape=jax.ShapeDtypeStruct(x.shape, F32),
        compiler_params=_params(2),
        name="ffn",
    )(*args)


def _pool(pz, halo, t_idx, n_tiles, seq, poolw_ref, pscale_ref):
    tm = pz.shape[0]
    prev = jnp.where(t_idx > 0, halo[:POOL_HALO], 0.0)
    nxt = jnp.where(t_idx < n_tiles - 1, halo[POOL_HALO:], 0.0)
    ext = jnp.concatenate([prev, pz, nxt], axis=0)
    n_ext = ext.shape[0]
    pos = t_idx * tm + lax.broadcasted_iota(jnp.int32, (tm, 1), 0)
    outs = []
    for g, w in enumerate(POOL_WINDOWS):
        sl = slice(g * POOL_GROUP, (g + 1) * POOL_GROUP)
        run = ext[:, sl]
        span = 1
        while span < w:
            run = run + pltpu.roll(run, n_ext - span, axis=0)
            span *= 2
        start = POOL_HALO - w // 2
        if start:
            run = pltpu.roll(run, n_ext - start, axis=0)
        win = run[:tm]
        cnt = jnp.minimum(pos + (w - w // 2), seq) - jnp.maximum(pos - w // 2, 0)
        d = (win / cnt.astype(F32) - pz[:, sl]).astype(BF16)
        outs.append(_dot(d, poolw_ref[g]))
    return jnp.concatenate(outs, axis=-1) * pscale_ref[...]


def _even_in_kernel(x_ref, xp_ref, xn_ref, sh_ref, sc_ref, ng_ref, wz_ref, gq_ref, wuq_ref, gkv_ref, wkv_ref,
                    tab_ref, poolw_ref, pscale_ref, q_ref, k_ref, v_ref, ob_ref, *, seq):
    t_idx = pl.program_id(1)
    n_tiles = pl.num_programs(1)
    g, sc, sh = ng_ref[...], sc_ref[...], sh_ref[...]
    h = _modnorm(x_ref[0], g, sc, sh).astype(BF16)
    z = _dot(h, wz_ref[...])
    o_kpe = MLA_Q_RANK + MLA_KV_RANK
    o_pz = o_kpe + LANES

    xh = jnp.concatenate([xp_ref[0], xn_ref[0]], axis=0)
    hh = _modnorm(xh, g, sc, sh).astype(BF16)
    halo = _dot(hh, wz_ref[:, o_pz:])
    ob_ref[0] = _pool(z[:, o_pz:], halo, t_idx, n_tiles, seq, poolw_ref, pscale_ref).astype(BF16)

    cq = (_rms(z[:, :MLA_Q_RANK]) * gq_ref[...]).astype(BF16)
    ckv = (_rms(z[:, MLA_Q_RANK:o_kpe]) * gkv_ref[...]).astype(BF16)
    q_all = _dot(cq, wuq_ref[...])
    kv_all = _dot(ckv, wkv_ref[...])
    kpe = _rope(z[:, o_kpe:o_pz], tab_ref, MLA_ROPE // 2)
    for hd in range(MLA_HEADS):
        sl = slice(hd * LANES, (hd + 1) * LANES)
        q_ref[0, hd, 0] = (_rope(q_all[:, sl], tab_ref, MLA_ROPE // 2) * MLA_QSCALE).T.astype(BF16)
        k_ref[0, hd] = (kv_all[:, sl] + kpe).astype(BF16)
    _store_value_chunks(v_ref, kv_all[:, MLA_HEADS * LANES:], MLA_V)


def _even_in_call(x, mod, layer, norm_g_row, wz, gq, wuq, gkv, wkv, tab, poolw, pscale):
    bsz, seq, _ = x.shape
    tm = TOKEN_TILE
    hb = tm // POOL_HALO
    n_hblk = seq // POOL_HALO
    tok = pl.BlockSpec((1, tm, D_MODEL), lambda b, t: (b, t, 0))
    prev = pl.BlockSpec((1, POOL_HALO, D_MODEL), lambda b, t: (b, jnp.maximum(t * hb - 1, 0), 0))
    nxt = pl.BlockSpec((1, POOL_HALO, D_MODEL), lambda b, t: (b, jnp.minimum((t + 1) * hb, n_hblk - 1), 0))
    heads = pl.BlockSpec((1, MLA_HEADS, tm, LANES), lambda b, t: (b, 0, t, 0))
    heads_t, out_heads_t = _query_block_out(bsz, seq, MLA_HEADS, _query_tile(seq, 2))
    half = pl.BlockSpec((1, tm, D_MODEL // 2), lambda b, t: (b, t, 0))
    out_heads = jax.ShapeDtypeStruct((bsz, MLA_HEADS, seq, LANES), BF16)
    out_half = jax.ShapeDtypeStruct((bsz, seq, D_MODEL // 2), BF16)
    vt_spec, vt_shape = _value_chunk_out(bsz, seq, MLA_HEADS, MLA_V)
    return pl.pallas_call(
        functools.partial(_even_in_kernel, seq=seq),
        grid=(bsz, seq // tm),
        in_specs=[tok, prev, nxt, _mod_spec(layer, 3), _mod_spec(layer, 4), _const_spec(norm_g_row.shape),
                  _const_spec(wz.shape), _const_spec(gq.shape), _const_spec(wuq.shape), _const_spec(gkv.shape),
                  _const_spec(wkv.shape), pl.BlockSpec((3, tm, LANES), lambda b, t: (0, t, 0)),
                  _const_spec(poolw.shape), _const_spec(pscale.shape)],
        out_specs=[heads_t, heads, vt_spec, half],
        out_shape=[out_heads_t, out_heads, vt_shape, out_half],
        compiler_params=_params(2),
        name="even_in",
    )(x, x, x, mod, mod, norm_g_row, wz, gq, wuq, gkv, wkv, tab, poolw, pscale)


def _pair_norm(x, gain):
    lo_mask = lax.broadcasted_iota(jnp.int32, x.shape, 1) < GQA_DIM
    sq = x * x
    lo = jnp.sum(jnp.where(lo_mask, sq, 0.0), axis=-1, keepdims=True)
    hi = jnp.sum(jnp.where(lo_mask, 0.0, sq), axis=-1, keepdims=True)
    inv = jnp.where(lo_mask, lax.rsqrt(lo / GQA_DIM + EPS), lax.rsqrt(hi / GQA_DIM + EPS))
    return x * inv * gain


def _odd_in_kernel(x_ref, sh_ref, sc_ref, ng_ref, wz_ref, gq_ref, gk_ref, tabc_ref, tabd_ref,
                   qc_ref, kc_ref, vc_ref, qd_ref, kd_ref, vd_ref):
    h = _modnorm(x_ref[0], ng_ref[...], sc_ref[...], sh_ref[...]).astype(BF16)
    z = _dot(h, wz_ref[...])
    n_qc = GQA_HEADS * GQA_DIM // LANES
    n_kv = GQA_KV_HEADS
    n_d = DIFF_HEADS
    col = 0
    for p in range(n_qc):
        blk = _rope(_pair_norm(z[:, col:col + LANES], gq_ref[...]), tabc_ref, GQA_DIM // 4)
        qc_ref[0, p, 0] = (blk * GQA_QSCALE).T.astype(BF16)
        col += LANES
    for j in range(n_kv):
        blk = z[:, col:col + LANES]
        kc_ref[0, j] = _rope(_pair_norm(blk, gk_ref[...]), tabc_ref, GQA_DIM // 4).astype(BF16)
        col += LANES
    _store_value_chunks(vc_ref, z[:, col:col + n_kv * GQA_DIM], GQA_DIM)
    col += n_kv * GQA_DIM
    for hd in range(n_d):
        qd_ref[0, hd, 0] = (_rope(z[:, col:col + LANES], tabd_ref, DIFF_ROPE // 2) * DIFF_QSCALE).T.astype(BF16)
        col += LANES
    for hd in range(n_d):
        kd_ref[0, hd] = _rope(z[:, col:col + LANES], tabd_ref, DIFF_ROPE // 2).astype(BF16)
        col += LANES
    _store_value_chunks(vd_ref, z[:, col:], 2 * DIFF_DIM)


def _odd_in_call(x, mod, layer, norm_g_row, wz, gq2, gk2, tabc, tabd):
    bsz, seq, _ = x.shape
    tm = TOKEN_TILE
    tok = pl.BlockSpec((1, tm, D_MODEL), lambda b, t: (b, t, 0))
    tab = pl.BlockSpec((3, tm, LANES), lambda b, t: (0, t, 0))

    def blocks(n):
        return (pl.BlockSpec((1, n, tm, LANES), lambda b, t: (b, 0, t, 0)),
                jax.ShapeDtypeStruct((bsz, n, seq, LANES), BF16))

    n_qc = GQA_HEADS * GQA_DIM // LANES
    outs = [_query_block_out(bsz, seq, n_qc, _query_tile(seq, 2 * n_qc // GQA_KV_HEADS)), blocks(GQA_KV_HEADS),
            _value_chunk_out(bsz, seq, GQA_KV_HEADS, GQA_DIM),
            _query_block_out(bsz, seq, DIFF_HEADS, _query_tile(seq, 2)), blocks(DIFF_HEADS),
            _value_chunk_out(bsz, seq, DIFF_HEADS, 2 * DIFF_DIM)]
    return pl.pallas_call(
        _odd_in_kernel,
        grid=(bsz, seq // tm),
        in_specs=[tok, _mod_spec(layer, 3), _mod_spec(layer, 4), _const_spec(norm_g_row.shape),
                  _const_spec(wz.shape), _const_spec(gq2.shape), _const_spec(gk2.shape), tab, tab],
        out_specs=[o[0] for o in outs],
        out_shape=[o[1] for o in outs],
        compiler_params=_params(2),
        name="odd_in",
    )(x, mod, mod, norm_g_row, wz, gq2, gk2, tabc, tabd)


def _attention_pipeline(n_units, n_chunks, lanes, load_q, stream_refs, finalize, q_scr, s_bufs, p_bufs, acc_ref):
    tk, n = s_bufs[0].shape
    assert n_chunks % 2 == 0 and n_chunks >= 2

    def cat(parts):
        return jnp.concatenate(parts, axis=1) if len(parts) > 1 else parts[0]

    def scores(u, j, s_ref):
        off = pl.multiple_of(j * tk, tk)
        cms = []
        for (k_ref, _), ln in zip(stream_refs(u), lanes):
            s = _dot(k_ref[pl.ds(off, tk), :], q_scr[:, ln])
            s_ref[:, ln] = s
            cms.append(jnp.max(s, axis=0, keepdims=True))
        return cat(cms)

    def values(u, j, p_ref, alpha):
        return cat([alpha[:, ln] * acc_ref[:, ln] + _dot(vt_ref[j], p_ref[:, ln])
                    for (_, vt_ref), ln in zip(stream_refs(u), lanes)])

    def exponentials(cur, m_new):
        for r in range(0, tk, EXP_ROWS):
            p_bufs[cur][r:r + EXP_ROWS, :] = jnp.exp2(s_bufs[cur][r:r + EXP_ROWS, :] - m_new).astype(BF16)

    def step(u, j, carry, cur, other):
        m, cm, alpha_prev = carry
        m_new = jnp.maximum(m, cm)
        alpha = jnp.exp2(m - m_new)
        cm_next = scores(u, j + 1, s_bufs[other])
        acc_ref[...] = values(u, j - 1, p_bufs[other], alpha_prev)
        exponentials(cur, m_new)
        return m_new, cm_next, alpha

    def first_step(u, cm):
        cm1 = scores(u, 1, s_bufs[1])
        acc_ref[...] = jnp.zeros(acc_ref.shape, F32)
        exponentials(0, cm)
        return cm, cm1, jnp.ones((1, n), F32)

    def unit(u, carry):
        def body(jj, carry):
            carry = step(u, 2 * jj + 1, carry, 1, 0)
            return step(u, 2 * jj + 2, carry, 0, 1)

        n_inner = (n_chunks - 2) // 2 + jnp.minimum(pl.program_id(0), 0)
        m, cm_last, alpha_prev = lax.fori_loop(0, n_inner, body, carry)
        m_new = jnp.maximum(m, cm_last)
        alpha = jnp.exp2(m - m_new)
        nxt = jnp.minimum(u + 1, n_units - 1)
        q_scr[...] = load_q(nxt)
        cm_next = scores(nxt, 0, s_bufs[0])
        acc_ref[...] = values(u, n_chunks - 2, p_bufs[0], alpha_prev)
        exponentials(1, m_new)
        cm1 = scores(nxt, 1, s_bufs[1])
        finalize(u, values(u, n_chunks - 1, p_bufs[1], alpha))
        acc_ref[...] = jnp.zeros(acc_ref.shape, F32)
        exponentials(0, cm_next)
        return cm_next, cm1, jnp.ones((1, n), F32)

    q_scr[...] = load_q(0)
    lax.fori_loop(0, n_units, unit, first_step(0, scores(0, 0, s_bufs[0])))


def _normalised(acc, rows):
    return acc[:rows] * (1.0 / acc[rows:rows + 1])


def _row_lo(shape):
    return lax.broadcasted_iota(jnp.int32, shape, 0) < LANES // 2


def _split_halves_t(qp):
    lo = _row_lo(qp.shape)
    return jnp.concatenate([jnp.where(lo, qp, 0), jnp.where(lo, 0, qp)], axis=1).astype(BF16)


def _attn_call(name, kernel_fn, qt, k, vt, extra, heads_per_step, q_blocks_per_step, out_lanes):
    bsz, n_qblk, n_q, _, tq = qt.shape
    seq = k.shape[2]
    n_groups = n_qblk // q_blocks_per_step
    k_per, v_per = k.shape[1] // n_groups, vt.shape[1] // n_groups
    nb = max(1, min(bsz, ATTN_KEYS_PER_STEP // seq))
    assert bsz % nb == 0
    n_lanes = heads_per_step * tq
    rows, tk = vt.shape[3:]
    return pl.pallas_call(
        functools.partial(kernel_fn, n_units=nb * n_q, n_q=n_q),
        grid=(bsz // nb, n_groups),
        in_specs=[pl.BlockSpec((nb, q_blocks_per_step, n_q, LANES, tq), lambda b, g: (b, g, 0, 0, 0)),
                  pl.BlockSpec((nb, k_per, seq, LANES), lambda b, g: (b, g, 0, 0)),
                  pl.BlockSpec((nb, v_per) + vt.shape[2:], lambda b, g: (b, g, 0, 0, 0))]
        + [_const_spec(e.shape) for e in extra],
        out_specs=pl.BlockSpec((nb, seq, out_lanes), lambda b, g: (b, 0, g)),
        out_shape=jax.ShapeDtypeStruct((bsz, seq, D_MODEL // 2), BF16),
        scratch_shapes=[pltpu.VMEM((LANES, n_lanes), BF16), [pltpu.VMEM((tk, n_lanes), F32)] * 2,
                        [pltpu.VMEM((tk, n_lanes), BF16)] * 2, pltpu.VMEM((rows, n_lanes), F32)],
        compiler_params=_params(2),
        name=name,
    )(qt, k, vt, *extra)


def _mla_attn_kernel(qt_ref, k_ref, vt_ref, o_ref, q_scr, s_bufs, p_bufs, acc_ref, *, n_units, n_q):
    tq = qt_ref.shape[4]

    def finalize(u, acc):
        o = _normalised(acc, MLA_V)
        o_ref[u // n_q, pl.ds(pl.multiple_of((u % n_q) * tq, tq), tq), :] = (
            jnp.concatenate([o[:, :tq], o[:, tq:]], axis=0).T.astype(BF16))

    _attention_pipeline(
        n_units, vt_ref.shape[2], [slice(0, tq), slice(tq, 2 * tq)],
        lambda u: jnp.concatenate([qt_ref[u // n_q, h, u % n_q] for h in range(2)], axis=1),
        lambda u: [(k_ref.at[u // n_q, h], vt_ref.at[u // n_q, h]) for h in range(2)],
        finalize, q_scr, s_bufs, p_bufs, acc_ref)


def _mla_attn_call(qt, k, vt):
    return _attn_call("mla_attn", _mla_attn_kernel, qt, k, vt, [], 2, 2, LANES)


def _gqa_attn_kernel(qt_ref, k_ref, vt_ref, o_ref, q_scr, s_bufs, p_bufs, acc_ref, *, n_units, n_q):
    n_pairs, tq = qt_ref.shape[1], qt_ref.shape[4]

    def finalize(u, acc):
        o = _normalised(acc, GQA_DIM)
        rows = pl.ds(pl.multiple_of((u % n_q) * tq, tq), tq)
        for p in range(n_pairs):
            pair = jnp.concatenate([o[:, (2 * p) * tq:(2 * p + 1) * tq], o[:, (2 * p + 1) * tq:(2 * p + 2) * tq]],
                                   axis=0)
            o_ref[u // n_q, rows, p * LANES:(p + 1) * LANES] = pair.T.astype(BF16)

    _attention_pipeline(
        n_units, vt_ref.shape[2], [slice(0, 2 * n_pairs * tq)],
        lambda u: jnp.concatenate([_split_halves_t(qt_ref[u // n_q, p, u % n_q]) for p in range(n_pairs)], axis=1),
        lambda u: [(k_ref.at[u // n_q, 0], vt_ref.at[u // n_q, 0])],
        finalize, q_scr, s_bufs, p_bufs, acc_ref)


def _gqa_attn_call(qt, k, vt):
    per_kv = qt.shape[1] // GQA_KV_HEADS
    return _attn_call("gqa_attn", _gqa_attn_kernel, qt, k, vt, [], 2 * per_kv, per_kv, per_kv * LANES)


def _diff_attn_kernel(qt_ref, k_ref, vt_ref, lq1_ref, lk1_ref, lq2_ref, lk2_ref, sg_ref, o_ref, q_scr, s_bufs, p_bufs,
                      acc_ref, *, n_units, n_q, lam_init):
    tq = qt_ref.shape[4]

    def finalize(u, acc):
        o = _normalised(acc, 2 * DIFF_DIM)
        lam = (jnp.exp(jnp.sum(lq1_ref[...] * lk1_ref[...], axis=-1, keepdims=True))
               - jnp.exp(jnp.sum(lq2_ref[...] * lk2_ref[...], axis=-1, keepdims=True)) + lam_init)
        od = o[:, :tq] - lam * o[:, tq:]
        od = od * lax.rsqrt(jnp.mean(od * od, axis=0, keepdims=True) + EPS)
        o_ref[u // n_q, pl.ds(pl.multiple_of((u % n_q) * tq, tq), tq), :] = (
            ((od * sg_ref[...]) * (1.0 - lam_init)).T.astype(BF16))

    _attention_pipeline(
        n_units, vt_ref.shape[2], [slice(0, 2 * tq)],
        lambda u: _split_halves_t(qt_ref[u // n_q, 0, u % n_q]),
        lambda u: [(k_ref.at[u // n_q, 0], vt_ref.at[u // n_q, 0])],
        finalize, q_scr, s_bufs, p_bufs, acc_ref)


def _diff_attn_call(qt, k, vt, lq1, lk1, lq2, lk2, subln_g_col, lam_init):
    return _attn_call("diff_attn", functools.partial(_diff_attn_kernel, lam_init=lam_init), qt, k, vt,
                      [lq1, lk1, lq2, lk2, subln_g_col], 2, 1, LANES)


def _prep_even(ev_w_in, mla_w_uq, mla_w_ukv):
    o_kpe = MLA_Q_RANK + MLA_KV_RANK
    wz = jnp.concatenate([
        ev_w_in[:, :o_kpe],
        jnp.zeros((D_MODEL, MLA_NOPE), F32), ev_w_in[:, o_kpe:o_kpe + MLA_ROPE],
        jnp.zeros((D_MODEL, LANES - MLA_NOPE - MLA_ROPE), F32),
        ev_w_in[:, o_kpe + MLA_ROPE:]], axis=1)
    wuq = mla_w_uq.reshape(MLA_Q_RANK, MLA_HEADS, MLA_NOPE + MLA_ROPE)
    wuq = jnp.pad(wuq, ((0, 0), (0, 0), (0, LANES - MLA_NOPE - MLA_ROPE))).reshape(MLA_Q_RANK, MLA_HEADS * LANES)
    wukv = mla_w_ukv.reshape(MLA_KV_RANK, MLA_HEADS, MLA_NOPE + MLA_V)
    wuk = jnp.pad(wukv[:, :, :MLA_NOPE], ((0, 0), (0, 0), (0, LANES - MLA_NOPE)))
    wkv = jnp.concatenate([wuk.reshape(MLA_KV_RANK, MLA_HEADS * LANES),
                           wukv[:, :, MLA_NOPE:].reshape(MLA_KV_RANK, MLA_HEADS * MLA_V)], axis=1)
    return wz.astype(BF16), wuq.astype(BF16), wkv.astype(BF16)


def _prep_odd(od_w_in):
    n_qc = GQA_HEADS * GQA_DIM
    n_kv = GQA_KV_HEADS * GQA_DIM

    def dup(w):
        w = w.reshape(D_MODEL, GQA_KV_HEADS, 1, GQA_DIM)
        return jnp.broadcast_to(w, (D_MODEL, GQA_KV_HEADS, 2, GQA_DIM)).reshape(D_MODEL, 2 * n_kv)

    wz = jnp.concatenate([od_w_in[:, :n_qc], dup(od_w_in[:, n_qc:n_qc + n_kv]), od_w_in[:, n_qc + n_kv:]], axis=1)
    return wz.astype(BF16)


def _rope_tables(seq):
    pos = jnp.arange(seq)

    def angles(p, n_freq, theta):
        inv = theta ** (-jnp.arange(0, n_freq, dtype=F32) * 2.0 / (2 * n_freq))
        ang = p.astype(F32)[:, None] * inv[None, :]
        return jnp.cos(ang), jnp.sin(ang)

    one = lambda n: jnp.ones((seq, n), F32)
    zero = lambda n: jnp.zeros((seq, n), F32)

    def stack(c, s1, s2):
        reps = LANES // c.shape[1]
        return jnp.stack([jnp.tile(t, (1, reps)) for t in (c, s1, s2)])

    cos, sin = angles(pos, MLA_ROPE // 2, ROPE_THETA)
    pad = LANES - MLA_NOPE - MLA_ROPE
    mla = stack(jnp.concatenate([one(MLA_NOPE), cos, cos, one(pad)], 1),
                jnp.concatenate([zero(MLA_NOPE), -sin, zero(MLA_ROPE // 2), zero(pad)], 1),
                jnp.concatenate([zero(MLA_NOPE), zero(MLA_ROPE // 2), sin, zero(pad)], 1))
    q = GQA_DIM // 4
    cr, sr = angles(pos // GRID_W, q, AXIAL_THETA)
    cc, sc = angles(pos % GRID_W, q, AXIAL_THETA)
    axial = stack(jnp.concatenate([cr, cr, cc, cc], 1),
                  jnp.concatenate([-sr, zero(q), -sc, zero(q)], 1),
                  jnp.concatenate([zero(q), sr, zero(q), sc], 1))
    hr = DIFF_ROPE // 2
    cd, sd = angles(pos, hr, ROPE_THETA)
    rest = DIFF_DIM - DIFF_ROPE
    diff = stack(jnp.concatenate([cd, cd, one(rest)], 1),
                 jnp.concatenate([-sd, zero(hr), zero(rest)], 1),
                 jnp.concatenate([zero(hr), sd, zero(rest)], 1))
    return mla, axial, diff


def kernel(x_prompt, x_sample, c_prompt, c_sample, w_ada, b_ada, norm_g, ffn_w_in, ffn_w_out, ev_w_in, mla_gq, mla_w_uq, mla_gkv, mla_w_ukv, pool_w, pool_scale, ev_w_out, od_w_in, gqa_gq, gqa_gk, diff_lq1, diff_lk1, diff_lq2, diff_lk2, diff_subln_g, od_w_out, final_g):
    groups = [x_prompt, x_sample]
    n_rows = [x.shape[0] for x in groups]
    c_all = jnp.concatenate([c_prompt, c_sample], axis=0)
    c_all = jnp.pad(c_all, ((0, -c_all.shape[0] % 16), (0, 0)))
    mod = _ada_call(c_all, w_ada, b_ada)
    mods, r0 = [], 0
    for n in n_rows:
        mods.append(mod[:, r0:r0 + n].reshape(DEPTH, n, 1, 9 * D_MODEL))
        r0 += n

    ffn_in = ffn_w_in.astype(BF16)
    ffn_out = ffn_w_out.astype(BF16)
    row = lambda v: v.reshape(1, -1)
    tables = [_rope_tables(x.shape[1]) for x in groups]

    xs = list(groups)
    for i in range(DEPTH):
        j = i // 2
        if i % 2 == 0:
            wz, wuq, wkv = _prep_even(ev_w_in[j], mla_w_uq[j], mla_w_ukv[j])
            w_mix = ev_w_out[j].astype(BF16)
            poolw = pool_w[j].astype(BF16)
        else:
            wz = _prep_odd(od_w_in[j])
            w_mix = od_w_out[j].astype(BF16)
            gq2 = row(jnp.tile(gqa_gq[j], 2))
            gk2 = row(jnp.tile(gqa_gk[j], 2))
            lam_init = 0.8 - 0.6 * math.exp(-0.3 * i)
        for gi in range(len(groups)):
            x, md = xs[gi], mods[gi]
            tab_mla, tab_axial, tab_diff = tables[gi]
            x = _ffn_call(x, md, i, 0, row(norm_g[i, 0]), ffn_in, ffn_out)
            if i % 2 == 0:
                q, k, v, ob = _even_in_call(x, md, i, row(norm_g[i, 1]), wz, row(mla_gq[j]), wuq, row(mla_gkv[j]),
                                            wkv, tab_mla, poolw, row(pool_scale[j]))
                mix = (_mla_attn_call(q, k, v), ob, w_mix)
            else:
                qc, kc, vc, qd, kd, vd = _odd_in_call(x, md, i, row(norm_g[i, 1]), wz, gq2, gk2, tab_axial, tab_diff)
                oc = _gqa_attn_call(qc, kc, vc)
                od = _diff_attn_call(qd, kd, vd, row(diff_lq1[j]), row(diff_lk1[j]), row(diff_lq2[j]),
                                     row(diff_lk2[j]), diff_subln_g[j].reshape(-1, 1), lam_init)
                mix = (oc, od, w_mix)
            xs[gi] = _ffn_call(x, md, i, 1, row(norm_g[i, 2]), ffn_in, ffn_out, mix=mix,
                               final_g=row(final_g) if i == DEPTH - 1 else None)
    return tuple(xs)
```
